```python
import math
import jax, jax.numpy as jnp
from jax import lax
import numpy as np

D_MODEL = 2048
BATCH = 1
SEQ = 8192
DEPTH = 4

N_MIXERS = 4
N_PER_MIXER = DEPTH // N_MIXERS
D_FF = 4 * D_MODEL
NORM_EPS = 1e-6
LN_EPS = 1e-5
CONV_WIDTH = 31
POOL_WINDOWS = (2, 4, 8, 16)
POOL_GROUP = D_MODEL // len(POOL_WINDOWS)
N_HEADS = 16
HEAD_DIM = D_MODEL // N_HEADS
N_KV_HEADS = 4
Q_PER_KV = N_HEADS // N_KV_HEADS
IDX_HEADS = 16
IDX_DIM = 64
TOPK_MAX = 256
Q_BLOCK = 128
ROPE_THETA = 500000.0
Q_WIDTH = N_HEADS * HEAD_DIM
KV_WIDTH = N_KV_HEADS * HEAD_DIM
QI_WIDTH = IDX_HEADS * IDX_DIM
ATTN_IN_WIDTH = Q_WIDTH + 2 * KV_WIDTH + QI_WIDTH + IDX_DIM + IDX_HEADS
ATTN_SPLITS = (Q_WIDTH, Q_WIDTH + KV_WIDTH, Q_WIDTH + 2 * KV_WIDTH,
               Q_WIDTH + 2 * KV_WIDTH + QI_WIDTH, Q_WIDTH + 2 * KV_WIDTH + QI_WIDTH + IDX_DIM)
SSM_GROUP = 16
SSM_GROUPS = D_MODEL // SSM_GROUP
SSM_STATE = 64

kernel_name = 'hybrid_interleaved_conv_pool_dsa_s5'


def _rmsnorm(x, g):
    xf = x.astype(jnp.float32)
    y = xf * lax.rsqrt(jnp.mean(xf * xf, axis=-1, keepdims=True) + NORM_EPS)
    return (y * g.astype(jnp.float32)).astype(x.dtype)


def _layernorm(x, g, b):
    xf = x.astype(jnp.float32)
    mu = jnp.mean(xf, axis=-1, keepdims=True)
    xc = xf - mu
    var = jnp.mean(xc * xc, axis=-1, keepdims=True)
    y = xc * lax.rsqrt(var + LN_EPS) * g.astype(jnp.float32) + b.astype(jnp.float32)
    return y.astype(x.dtype)


def _rope_partial(x, pos):
    d = x.shape[-1]
    r = d // 4
    half = r // 2
    inv = jnp.power(ROPE_THETA, -2.0 * jnp.arange(half, dtype=jnp.float32) / r)
    ang = pos.astype(jnp.float32)[:, None] * inv[None, :]
    cos = jnp.cos(ang)[:, None, :].astype(x.dtype)
    sin = jnp.sin(ang)[:, None, :].astype(x.dtype)
    x1 = x[..., :half]
    x2 = x[..., half:r]
    return jnp.concatenate([x1 * cos - x2 * sin, x2 * cos + x1 * sin, x[..., r:]], axis=-1)


def _conv_module(x, w_in, b_in, w_dw, b_dw, ln_g, ln_b, w_out):
    h = x @ w_in + b_in
    a, g = jnp.split(h, 2, axis=-1)
    h = a * jax.nn.sigmoid(g)
    h = lax.conv_general_dilated(
        h, w_dw[:, None, :], window_strides=(1,), padding=[(CONV_WIDTH - 1, 0)],
        dimension_numbers=('NWC', 'WIO', 'NWC'), feature_group_count=D_MODEL) + b_dw
    h = jax.nn.silu(_layernorm(h, ln_g, ln_b))
    return h @ w_out


def _pool_mixer(x, pool_w, pool_scale):
    bsz, L, _ = x.shape
    xf = x.astype(jnp.float32)
    cs = jnp.concatenate([jnp.zeros((bsz, 1, D_MODEL), jnp.float32), lax.cumsum(xf, axis=1)], axis=1)
    t1 = jnp.arange(1, L + 1, dtype=jnp.float32)[None, :, None]
    outs = []
    for g, w in enumerate(POOL_WINDOWS):
        sl = slice(g * POOL_GROUP, (g + 1) * POOL_GROUP)
        csg = cs[:, :, sl]
        lower = jnp.concatenate([jnp.zeros((bsz, w - 1, POOL_GROUP), jnp.float32), csg[:, :L - w + 1]], axis=1)
        mean = (csg[:, 1:] - lower) / jnp.minimum(t1, float(w))
        outs.append(mean - xf[:, :, sl])
    pooled = jnp.stack(outs, axis=2).astype(x.dtype)
    mixed = jnp.einsum('blgc,gce->blge', pooled, pool_w).reshape(bsz, L, D_MODEL)
    return mixed * pool_scale


def _sparse_attention(x, w_in, w_out):
    bsz, L, _ = x.shape
    proj = x @ w_in
    q, k, v, qi, ki, wi = jnp.split(proj, ATTN_SPLITS, axis=-1)
    pos = jnp.arange(L)
    q = _rope_partial(q.reshape(bsz, L, N_HEADS, HEAD_DIM), pos)
    k = _rope_partial(k.reshape(bsz, L, N_KV_HEADS, HEAD_DIM), pos)
    v = v.reshape(bsz, L, N_KV_HEADS, HEAD_DIM)
    qi = _rope_partial(qi.reshape(bsz, L, IDX_HEADS, IDX_DIM), pos)
    ki = _rope_partial(ki.reshape(bsz, L, 1, IDX_DIM), pos)[:, :, 0]
    wi = wi * (IDX_HEADS ** -0.5 * IDX_DIM ** -0.5)
    topk = min(TOPK_MAX, L // 4)
    nblk = L // Q_BLOCK

    def to_blocks(a):
        return jnp.moveaxis(a.reshape(bsz, nblk, Q_BLOCK, *a.shape[2:]), 1, 0)

    def block(args):
        qb, qib, wib, start = args
        tq = start + jnp.arange(Q_BLOCK)
        sc = jnp.einsum('bqhd,bsd->bqhs', qib, ki).astype(jnp.float32)
        I = jnp.einsum('bqhs,bqh->bqs', jax.nn.relu(sc), wib.astype(jnp.float32))
        causal = pos[None, :] <= tq[:, None]
        I = jnp.where(causal[None], I, -jnp.inf)
        _, idx = lax.top_k(I, topk)
        valid = idx <= tq[None, :, None]
        kg = jax.vmap(lambda kb, ib: kb[ib])(k, idx)
        vg = jax.vmap(lambda vb, ib: vb[ib])(v, idx)
        qg = qb.reshape(bsz, Q_BLOCK, N_KV_HEADS, Q_PER_KV, HEAD_DIM)
        s = jnp.einsum('bqngd,bqknd->bqngk', qg, kg).astype(jnp.float32) * (HEAD_DIM ** -0.5)
        s = jnp.where(valid[:, :, None, None, :], s, -jnp.inf)
        p = jax.nn.softmax(s, axis=-1).astype(vg.dtype)
        o = jnp.einsum('bqngk,bqknd->bqngd', p, vg)
        return o.reshape(bsz, Q_BLOCK, Q_WIDTH)

    out = lax.map(block, (to_blocks(q), to_blocks(qi), to_blocks(wi), jnp.arange(nblk) * Q_BLOCK))
    out = jnp.moveaxis(out, 0, 1).reshape(bsz, L, Q_WIDTH)
    return out @ w_out


def _complex_affine_combine(e1, e2):
    a1r, a1i, b1r, b1i = e1
    a2r, a2i, b2r, b2i = e2
    ar = a2r * a1r - a2i * a1i
    ai = a2r * a1i + a2i * a1r
    br = a2r * b1r - a2i * b1i + b2r
    bi = a2r * b1i + a2i * b1r + b2i
    return (ar, ai, br, bi)


def _s5_mixer(u, lam_re, lam_im, log_dt, b_re, b_im, c_re, c_im, d_skip, w_glu):
    bsz, L, _ = u.shape
    uf = u.astype(jnp.float32)
    ug = uf.reshape(bsz, L, SSM_GROUPS, SSM_GROUP)
    lr = lam_re.astype(jnp.float32)
    li = lam_im.astype(jnp.float32)
    dt = jnp.exp(log_dt.astype(jnp.float32))[:, None]
    mag = jnp.exp(lr * dt)
    ar = mag * jnp.cos(li * dt)
    ai = mag * jnp.sin(li * dt)
    den = lr * lr + li * li
    cr = ((ar - 1.0) * lr + ai * li) / den
    ci = (ai * lr - (ar - 1.0) * li) / den
    br = b_re.astype(jnp.float32)
    bi = b_im.astype(jnp.float32)
    bbr = cr[..., None] * br - ci[..., None] * bi
    bbi = cr[..., None] * bi + ci[..., None] * br
    bu_r = jnp.einsum('blgc,gpc->blgp', ug, bbr)
    bu_i = jnp.einsum('blgc,gpc->blgp', ug, bbi)
    a_r = jnp.broadcast_to(ar, bu_r.shape)
    a_i = jnp.broadcast_to(ai, bu_i.shape)
    _, _, h_r, h_i = lax.associative_scan(_complex_affine_combine, (a_r, a_i, bu_r, bu_i), axis=1)
    y = (jnp.einsum('blgp,gcp->blgc', h_r, c_re.astype(jnp.float32))
         - jnp.einsum('blgp,gcp->blgc', h_i, c_im.astype(jnp.float32)))
    y = y.reshape(bsz, L, D_MODEL) + d_skip.astype(jnp.float32) * uf
    y = jax.nn.gelu(y).astype(u.dtype)
    z = y @ w_glu
    a, g = jnp.split(z, 2, axis=-1)
    return a * jax.nn.sigmoid(g)


def _sq_relu_mlp(x, w_up, w_down):
    h = jax.nn.relu(x @ w_up)
    return (h * h) @ w_down


def setup_inputs(seed: int = 0) -> dict:
    key = jax.random.key(seed)
    ks = jax.random.split(key, 24)
    f32 = jnp.float32

    def nrm(k, shape, scale):
        return scale * jax.random.normal(k, shape, f32)

    P = N_PER_MIXER
    x = nrm(ks[0], (BATCH, SEQ, D_MODEL), 1.0)
    norm_gains = 1.0 + nrm(ks[1], (DEPTH, 4, D_MODEL), 0.05)
    mlp_w_up = nrm(ks[2], (DEPTH, D_MODEL, D_FF), D_MODEL ** -0.5)
    mlp_w_down = nrm(ks[3], (DEPTH, D_FF, D_MODEL), D_FF ** -0.5)
    conv_w_in = nrm(ks[4], (P, D_MODEL, 2 * D_MODEL), D_MODEL ** -0.5)
    conv_b_in = nrm(ks[5], (P, 2 * D_MODEL), 0.02)
    conv_w_dw = nrm(ks[6], (P, CONV_WIDTH, D_MODEL), CONV_WIDTH ** -0.5)
    conv_b_dw = nrm(ks[7], (P, D_MODEL), 0.02)
    conv_ln_g = 1.0 + nrm(ks[8], (P, D_MODEL), 0.05)
    conv_ln_b = nrm(ks[9], (P, D_MODEL), 0.02)
    conv_w_out = nrm(ks[10], (P, D_MODEL, D_MODEL), D_MODEL ** -0.5)
    pool_w = nrm(ks[11], (P, len(POOL_WINDOWS), POOL_GROUP, POOL_GROUP), POOL_GROUP ** -0.5)
    pool_scale = 1.0 + nrm(ks[12], (P, D_MODEL), 0.1)
    attn_w_in = nrm(ks[13], (P, D_MODEL, ATTN_IN_WIDTH), D_MODEL ** -0.5)
    attn_w_out = nrm(ks[14], (P, Q_WIDTH, D_MODEL), Q_WIDTH ** -0.5)
    n = jnp.arange(SSM_STATE, dtype=f32)
    ssm_lambda_re = -0.5 + nrm(ks[15], (P, SSM_GROUPS, SSM_STATE), 0.01)
    ssm_lambda_im = math.pi * n + nrm(ks[16], (P, SSM_GROUPS, SSM_STATE), 0.01)
    ssm_log_dt = jax.random.uniform(ks[17], (P, SSM_GROUPS), f32, math.log(1e-3), math.log(1e-1))
    ssm_b_re = nrm(ks[18], (P, SSM_GROUPS, SSM_STATE, SSM_GROUP), (2 * SSM_GROUP) ** -0.5)
    ssm_b_im = nrm(ks[19], (P, SSM_GROUPS, SSM_STATE, SSM_GROUP), (2 * SSM_GROUP) ** -0.5)
    ssm_c_re = nrm(ks[20], (P, SSM_GROUPS, SSM_GROUP, SSM_STATE), SSM_STATE ** -0.5)
    ssm_c_im = nrm(ks[21], (P, SSM_GROUPS, SSM_GROUP, SSM_STATE), SSM_STATE ** -0.5)
    ssm_d = nrm(ks[22], (P, D_MODEL), 1.0)
    ssm_w_glu = nrm(ks[23], (P, D_MODEL, 2 * D_MODEL), D_MODEL ** -0.5)
    return {'x': x, 'norm_gains': norm_gains, 'mlp_w_up': mlp_w_up, 'mlp_w_down': mlp_w_down,
            'conv_w_in': conv_w_in, 'conv_b_in': conv_b_in, 'conv_w_dw': conv_w_dw,
            'conv_b_dw': conv_b_dw, 'conv_ln_g': conv_ln_g, 'conv_ln_b': conv_ln_b,
            'conv_w_out': conv_w_out, 'pool_w': pool_w, 'pool_scale': pool_scale,
            'attn_w_in': attn_w_in, 'attn_w_out': attn_w_out,
            'ssm_lambda_re': ssm_lambda_re, 'ssm_lambda_im': ssm_lambda_im, 'ssm_log_dt': ssm_log_dt,
            'ssm_b_re': ssm_b_re, 'ssm_b_im': ssm_b_im, 'ssm_c_re': ssm_c_re, 'ssm_c_im': ssm_c_im,
            'ssm_d': ssm_d, 'ssm_w_glu': ssm_w_glu}


def reference(x, norm_gains, mlp_w_up, mlp_w_down, conv_w_in, conv_b_in, conv_w_dw, conv_b_dw,
              conv_ln_g, conv_ln_b, conv_w_out, pool_w, pool_scale, attn_w_in, attn_w_out,
              ssm_lambda_re, ssm_lambda_im, ssm_log_dt, ssm_b_re, ssm_b_im, ssm_c_re, ssm_c_im,
              ssm_d, ssm_w_glu):
    res = x
    for i in range(DEPTH):
        m = i % N_MIXERS
        j = i // N_MIXERS
        h = _rmsnorm(res, norm_gains[i, 0])
        if m == 0:
            h = _conv_module(h, conv_w_in[j], conv_b_in[j], conv_w_dw[j], conv_b_dw[j],
                             conv_ln_g[j], conv_ln_b[j], conv_w_out[j])
        elif m == 1:
            h = _pool_mixer(h, pool_w[j], pool_scale[j])
        elif m == 2:
            h = _sparse_attention(h, attn_w_in[j], attn_w_out[j])
        else:
            h = _s5_mixer(h, ssm_lambda_re[j], ssm_lambda_im[j], ssm_log_dt[j], ssm_b_re[j],
                          ssm_b_im[j], ssm_c_re[j], ssm_c_im[j], ssm_d[j], ssm_w_glu[j])
        res = res + _rmsnorm(h, norm_gains[i, 1])
        h = _rmsnorm(res, norm_gains[i, 2])
        h = _sq_relu_mlp(h, mlp_w_up[i], mlp_w_down[i])
        res = res + _rmsnorm(h, norm_gains[i, 3])
    return res
```

```python
import functools
import math

import jax
import jax.numpy as jnp
from jax import lax
from jax.experimental import pallas as pl
from jax.experimental.pallas import tpu as pltpu

NORM_EPS = 1e-6
LN_EPS = 1e-5
CONV_WIDTH = 31
POOL_WINDOWS = (2, 4, 8, 16)
N_HEADS = 16
N_KV_HEADS = 4
Q_PER_KV = N_HEADS // N_KV_HEADS
IDX_HEADS = 16
IDX_DIM = 64
TOPK_MAX = 256
ROPE_THETA = 500000.0
SSM_GROUP = 16
SSM_STATE = 64

V7X_LANES = 128
V7X_SUBLANES = 8
V7X_VMEM_BYTES = 64 * 1024 * 1024
VMEM_LIMIT = V7X_VMEM_BYTES - 8 * 1024 * 1024

INT_MIN = -(2 ** 31)
NEG_BIG = -1e30
F32 = jnp.float32
BF16 = jnp.bfloat16


def _cparams(*sem):
    return pltpu.CompilerParams(dimension_semantics=sem, vmem_limit_bytes=VMEM_LIMIT)


def _rms(x, g):
    ms = jnp.mean(x * x, axis=-1, keepdims=True)
    return x * lax.rsqrt(ms + NORM_EPS) * g


def _pick_tn(n, cap):
    return max(t for t in range(V7X_LANES, cap + 1, V7X_LANES) if n % t == 0)


def _dot(a, b):
    return jnp.dot(a, b, preferred_element_type=F32)


def _fm_body(*refs, has_norm, has_bias, glu, has_epi, nj, tn):
    it = iter(refs)
    x_ref = next(it)
    gin_ref = next(it) if has_norm else None
    w1_ref = next(it)
    w2_ref = next(it) if glu else None
    b1_ref = next(it) if has_bias else None
    b2_ref = next(it) if (has_bias and glu) else None
    res_ref = next(it) if has_epi else None
    gout_ref = next(it) if has_epi else None
    out_ref = next(it)
    xn_ref = next(it)
    acc_ref = next(it) if has_epi else None

    j = pl.program_id(1)

    @pl.when(j == 0)
    def _():
        x = x_ref[...].astype(F32)
        if has_norm:
            x = _rms(x, gin_ref[...])
        xn_ref[...] = x.astype(BF16)

    xn = xn_ref[...]
    y = _dot(xn, w1_ref[...])
    if has_bias:
        y = y + b1_ref[...]
    if glu:
        y2 = _dot(xn, w2_ref[...])
        if has_bias:
            y2 = y2 + b2_ref[...]
        y = y * jax.nn.sigmoid(y2)
    if not has_epi:
        out_ref[...] = y.astype(out_ref.dtype)
    else:
        acc_ref[j] = y

        @pl.when(j == nj - 1)
        def _():
            ssq = jnp.zeros((acc_ref.shape[1], 1), F32)
            for jj in range(nj):
                a = acc_ref[jj]
                ssq = ssq + jnp.sum(a * a, axis=-1, keepdims=True)
            rs = lax.rsqrt(ssq / (nj * tn) + NORM_EPS)
            for jj in range(nj):
                sl = slice(jj * tn, (jj + 1) * tn)
                out_ref[:, sl] = res_ref[:, sl] + acc_ref[jj] * rs * gout_ref[:, sl]


def fused_matmul(x, w1, *, w2=None, gin=None, b1=None, b2=None, res=None, gout=None,
                 out_dtype=F32, tm=512, tn=512, name="fused_matmul"):
    L, K = x.shape
    N = w1.shape[1]
    tn = min(tn, N)
    assert L % tm == 0 and N % tn == 0
    nj = N // tn
    has_norm, has_bias, glu, has_epi = gin is not None, b1 is not None, w2 is not None, res is not None

    args = [x]
    specs = [pl.BlockSpec((tm, K), lambda i, j: (i, 0))]
    if has_norm:
        args.append(gin.reshape(1, K).astype(F32))
        specs.append(pl.BlockSpec((1, K), lambda i, j: (0, 0)))
    args.append(w1)
    specs.append(pl.BlockSpec((K, tn), lambda i, j: (0, j)))
    if glu:
        args.append(w2)
        specs.append(pl.BlockSpec((K, tn), lambda i, j: (0, j)))
    if has_bias:
        args.append(b1.reshape(1, N).astype(F32))
        specs.append(pl.BlockSpec((1, tn), lambda i, j: (0, j)))
        if glu:
            args.append(b2.reshape(1, N).astype(F32))
            specs.append(pl.BlockSpec((1, tn), lambda i, j: (0, j)))
    scratch = [pltpu.VMEM((tm, K), BF16)]
    if has_epi:
        args += [res, gout.reshape(1, N).astype(F32)]
        specs += [pl.BlockSpec((tm, N), lambda i, j: (i, 0)), pl.BlockSpec((1, N), lambda i, j: (0, 0))]
        out_spec = pl.BlockSpec((tm, N), lambda i, j: (i, 0))
        out_dtype = F32
        scratch.append(pltpu.VMEM((nj, tm, tn), F32))
    else:
        out_spec = pl.BlockSpec((tm, tn), lambda i, j: (i, j))

    body = functools.partial(_fm_body, has_norm=has_norm, has_bias=has_bias, glu=glu,
                             has_epi=has_epi, nj=nj, tn=tn)
    return pl.pallas_call(
        body,
        grid=(L // tm, nj),
        in_specs=specs,
        out_specs=out_spec,
        out_shape=jax.ShapeDtypeStruct((L, N), out_dtype),
        scratch_shapes=scratch,
        compiler_params=_cparams("parallel", "arbitrary"),
        name=name,
    )(*args)


def _mlp_body(res_ref, gin_ref, wu_ref, wd_ref, gout_ref, out_ref, xn_ref, acc_ref, *, nj):
    j = pl.program_id(1)

    @pl.when(j == 0)
    def _():
        xn_ref[...] = _rms(res_ref[...], gin_ref[...]).astype(BF16)
        acc_ref[...] = jnp.zeros_like(acc_ref)

    h = _dot(xn_ref[...], wu_ref[...])
    h = jnp.maximum(h, 0.0)
    acc_ref[...] += _dot((h * h).astype(BF16), wd_ref[...])

    @pl.when(j == nj - 1)
    def _():
        out_ref[...] = res_ref[...] + _rms(acc_ref[...], gout_ref[...])


def mlp_layer(res, gin, w_up, w_down, gout, *, tm=512, tf=512):
    L, D = res.shape
    F = w_up.shape[1]
    nj = F // tf
    return pl.pallas_call(
        functools.partial(_mlp_body, nj=nj),
        grid=(L // tm, nj),
        in_specs=[
            pl.BlockSpec((tm, D), lambda i, j: (i, 0)),
            pl.BlockSpec((1, D), lambda i, j: (0, 0)),
            pl.BlockSpec((D, tf), lambda i, j: (0, j)),
            pl.BlockSpec((tf, D), lambda i, j: (j, 0)),
            pl.BlockSpec((1, D), lambda i, j: (0, 0)),
        ],
        out_specs=pl.BlockSpec((tm, D), lambda i, j: (i, 0)),
        out_shape=jax.ShapeDtypeStruct((L, D), F32),
        scratch_shapes=[pltpu.VMEM((tm, D), BF16), pltpu.VMEM((tm, D), F32)],
        compiler_params=_cparams("parallel", "arbitrary"),
        name="mlp",
    )(res, gin.reshape(1, D), w_up, w_down, gout.reshape(1, D))


CONV_HALO = 32
CONV_ROWS = 32
CONV_COLS = 512


def _dwconv_body(cur_ref, halo_ref, w_ref, b_ref, lg_ref, lb_ref, out_ref, ext_ref, cv_ref, *, tm):
    i = pl.program_id(0)
    D = cur_ref.shape[1]
    halo = halo_ref[...]
    ext_ref[0:CONV_HALO, :] = jnp.where(i == 0, jnp.zeros_like(halo), halo)
    ext_ref[CONV_HALO:, :] = cur_ref[...]
    for r0 in range(0, tm, CONV_ROWS):
        for c0 in range(0, D, CONV_COLS):
            cs = slice(c0, c0 + CONV_COLS)
            acc = jnp.zeros((CONV_ROWS, CONV_COLS), F32) + b_ref[:, cs]
            for k in range(CONV_WIDTH):
                start = CONV_HALO + r0 - (CONV_WIDTH - 1) + k
                acc = acc + ext_ref[start:start + CONV_ROWS, cs] * w_ref[k:k + 1, cs]
            cv_ref[r0:r0 + CONV_ROWS, cs] = acc
    h = cv_ref[...]
    mu = jnp.mean(h, axis=-1, keepdims=True)
    hc = h - mu
    var = jnp.mean(hc * hc, axis=-1, keepdims=True)
    y = hc * lax.rsqrt(var + LN_EPS) * lg_ref[...] + lb_ref[...]
    out_ref[...] = (y * jax.nn.sigmoid(y)).astype(out_ref.dtype)


def dwconv_ln_swish(h, w_dw, b_dw, ln_g, ln_b, *, tm=128):
    L, D = h.shape
    hb = tm // CONV_HALO
    return pl.pallas_call(
        functools.partial(_dwconv_body, tm=tm),
        grid=(L // tm,),
        in_specs=[
            pl.BlockSpec((tm, D), lambda i: (i, 0)),
            pl.BlockSpec((CONV_HALO, D), lambda i: (jnp.maximum(i * hb - 1, 0), 0)),
            pl.BlockSpec((CONV_WIDTH, D), lambda i: (0, 0)),
            pl.BlockSpec((1, D), lambda i: (0, 0)),
            pl.BlockSpec((1, D), lambda i: (0, 0)),
            pl.BlockSpec((1, D), lambda i: (0, 0)),
        ],
        out_specs=pl.BlockSpec((tm, D), lambda i: (i, 0)),
        out_shape=jax.ShapeDtypeStruct((L, D), BF16),
        scratch_shapes=[pltpu.VMEM((tm + CONV_HALO, D), F32), pltpu.VMEM((tm, D), F32)],
        compiler_params=_cparams("parallel"),
        name="dwconv_ln_swish",
    )(h, h, w_dw, b_dw.reshape(1, D), ln_g.reshape(1, D), ln_b.reshape(1, D))


POOL_HALO = 16


def _pool_body(cur_ref, halo_ref, gin_ref, pw_ref, ps_ref, gout_ref, out_ref, ext_ref, mix_ref, *, tm):
    i = pl.program_id(0)
    D = cur_ref.shape[1]
    cg = D // len(POOL_WINDOWS)
    hn = _rms(halo_ref[...], gin_ref[...])
    ext_ref[0:POOL_HALO, :] = jnp.where(i == 0, jnp.zeros_like(hn), hn)
    ext_ref[POOL_HALO:, :] = _rms(cur_ref[...], gin_ref[...])
    t1 = (i * tm + 1 + lax.broadcasted_iota(jnp.int32, (tm, 1), 0)).astype(F32)
    for g, w in enumerate(POOL_WINDOWS):
        cs = slice(g * cg, (g + 1) * cg)
        s = ext_ref[POOL_HALO:POOL_HALO + tm, cs]
        x = s
        for d in range(1, w):
            s = s + ext_ref[POOL_HALO - d:POOL_HALO - d + tm, cs]
        pooled = s / jnp.minimum(t1, float(w)) - x
        mix_ref[:, cs] = _dot(pooled.astype(BF16), pw_ref[g]) * ps_ref[:, cs]
    out_ref[...] = cur_ref[...] + _rms(mix_ref[...], gout_ref[...])


def pool_layer(res, gin, pool_w, pool_scale, gout, *, tm=256):
    L, D = res.shape
    ng = len(POOL_WINDOWS)
    cg = D // ng
    hb = tm // POOL_HALO
    return pl.pallas_call(
        functools.partial(_pool_body, tm=tm),
        grid=(L // tm,),
        in_specs=[
            pl.BlockSpec((tm, D), lambda i: (i, 0)),
            pl.BlockSpec((POOL_HALO, D), lambda i: (jnp.maximum(i * hb - 1, 0), 0)),
            pl.BlockSpec((1, D), lambda i: (0, 0)),
            pl.BlockSpec((ng, cg, cg), lambda i: (0, 0, 0)),
            pl.BlockSpec((1, D), lambda i: (0, 0)),
            pl.BlockSpec((1, D), lambda i: (0, 0)),
        ],
        out_specs=pl.BlockSpec((tm, D), lambda i: (i, 0)),
        out_shape=jax.ShapeDtypeStruct((L, D), F32),
        scratch_shapes=[pltpu.VMEM((tm + POOL_HALO, D), F32), pltpu.VMEM((tm, D), F32)],
        compiler_params=_cparams("parallel"),
        name="pool_layer",
    )(res, res, gin.reshape(1, D), pool_w, pool_scale.reshape(1, D), gout.reshape(1, D))


def _rope_tables(L, width, r):
    half = r // 2
    inv = jnp.power(ROPE_THETA, -2.0 * jnp.arange(half, dtype=F32) / r)
    ang = jnp.arange(L, dtype=F32)[:, None] * inv[None, :]
    cos, sin = jnp.cos(ang), jnp.sin(ang)
    pad = width - r
    cos_h = jnp.concatenate([cos, cos, jnp.ones((L, pad), F32)], axis=1)
    sin_h = jnp.concatenate([-sin, sin, jnp.zeros((L, pad), F32)], axis=1)
    reps = V7X_LANES // width
    return jnp.tile(cos_h, (1, reps)), jnp.tile(sin_h, (1, reps))


def _rope128(x, cos_t, sin_t, half, width):
    lane = lax.broadcasted_iota(jnp.int32, x.shape, 1) % width
    fwd = pltpu.roll(x, V7X_LANES - half, axis=1)
    bwd = pltpu.roll(x, half, axis=1)
    partner = jnp.where(lane < half, fwd, bwd)
    return x * cos_t + partner * sin_t


def _rope_body(p_ref, cq_ref, sq_ref, ci_ref, si_ref, q_ref, k_ref, v_ref, qi_ref, ki_ref, wi_ref,
               *, dq, dkv, dqi, q_scale, wi_scale):
    cq, sq, ci, si = cq_ref[...], sq_ref[...], ci_ref[...], si_ref[...]
    hd = V7X_LANES
    for h in range(dq // hd):
        x = p_ref[:, h * hd:(h + 1) * hd]
        q_ref[:, h * hd:(h + 1) * hd] = (_rope128(x, cq, sq, 16, hd) * q_scale).astype(BF16)
    o = dq
    for h in range(dkv // hd):
        x = p_ref[:, o + h * hd:o + (h + 1) * hd]
        k_ref[:, h * hd:(h + 1) * hd] = _rope128(x, cq, sq, 16, hd).astype(BF16)
    o += dkv
    v_ref[...] = p_ref[:, o:o + dkv].astype(BF16)
    o += dkv
    for h in range(dqi // hd):
        x = p_ref[:, o + h * hd:o + (h + 1) * hd]
        qi_ref[:, h * hd:(h + 1) * hd] = _rope128(x, ci, si, 8, IDX_DIM).astype(BF16)
    o += dqi
    x = p_ref[:, o:o + hd]
    xr = _rope128(x, ci, si, 8, IDX_DIM)
    ki_ref[...] = xr[:, :IDX_DIM].astype(BF16)
    wi_ref[...] = x[:, IDX_DIM:IDX_DIM + IDX_HEADS] * wi_scale


def rope_split(proj, dq, dkv, dqi, *, tm=256):
    L, W = proj.shape
    cq, sq = _rope_tables(L, V7X_LANES, 32)
    ci, si = _rope_tables(L, IDX_DIM, 16)
    tab = pl.BlockSpec((tm, V7X_LANES), lambda i: (i, 0))
    row = lambda n: pl.BlockSpec((tm, n), lambda i: (i, 0))
    body = functools.partial(_rope_body, dq=dq, dkv=dkv, dqi=dqi, q_scale=V7X_LANES ** -0.5,
                             wi_scale=IDX_HEADS ** -0.5 * IDX_DIM ** -0.5)
    return pl.pallas_call(
        body,
        grid=(L // tm,),
        in_specs=[row(W), tab, tab, tab, tab],
        out_specs=[row(dq), row(dkv), row(dkv), row(dqi), row(IDX_DIM), row(IDX_HEADS)],
        out_shape=[
            jax.ShapeDtypeStruct((L, dq), BF16), jax.ShapeDtypeStruct((L, dkv), BF16),
            jax.ShapeDtypeStruct((L, dkv), BF16), jax.ShapeDtypeStruct((L, dqi), BF16),
            jax.ShapeDtypeStruct((L, IDX_DIM), BF16), jax.ShapeDtypeStruct((L, IDX_HEADS), F32),
        ],
        compiler_params=_cparams("parallel"),
        name="rope_split",
    )(proj, cq, sq, ci, si)


IDX_TQ = 128
IDX_CK = 512


def _sortable(x):
    b = pltpu.bitcast(x + 0.0, jnp.int32)
    return jnp.where(b < 0, b ^ jnp.int32(0x7FFFFFFF), b)


def _index_body(qi_ref, kit_ref, wi_ref, mask_ref, keys_ref, jmax_ref, *, topk, nck):
    i = pl.program_id(0)
    tq, ck = IDX_TQ, IDX_CK
    n_act = (i * tq + tq + ck - 1) // ck
    tpos = i * tq + lax.broadcasted_iota(jnp.int32, (tq, 1), 0)
    wi = wi_ref[...]

    def fill(c, carry):
        kt = kit_ref[c]
        acc = jnp.zeros((tq, ck), F32)
        for h in range(IDX_HEADS):
            sc = _dot(qi_ref[:, h * IDX_DIM:(h + 1) * IDX_DIM], kt)
            acc = acc + jnp.maximum(sc, 0.0) * wi[:, h:h + 1]
        spos = c * ck + lax.broadcasted_iota(jnp.int32, (tq, ck), 1)
        keys_ref[c] = jnp.where(spos <= tpos, _sortable(acc), INT_MIN)
        return carry

    lax.fori_loop(0, n_act, fill, 0)

    def count(pred):
        def step(c, part):
            m = jnp.where(pred(keys_ref[c], c), 1, 0)
            for l0 in range(0, ck, V7X_LANES):
                part = part + m[:, l0:l0 + V7X_LANES]
            return part
        part = lax.fori_loop(0, n_act, step, jnp.zeros((tq, V7X_LANES), jnp.int32))
        return jnp.sum(part, axis=-1, keepdims=True)

    thr = jnp.where(count(lambda kc, c: kc >= 0) >= topk, jnp.int32(0), jnp.int32(INT_MIN))

    def bit_step(b, thr):
        cand = thr | (jnp.int32(1) << (30 - b))
        return jnp.where(count(lambda kc, c: kc >= cand) >= topk, cand, thr)

    thr = lax.fori_loop(0, 31, bit_step, thr)

    need = topk - count(lambda kc, c: kc > thr)
    n_eq = count(lambda kc, c: kc == thr)
    jmax_ref[...] = jnp.full(jmax_ref.shape, nck * ck, jnp.int32)

    @pl.when(jnp.max(n_eq - need) > 0)
    def _():
        def eq_upto(j):
            def pred(kc, c):
                spos = c * ck + lax.broadcasted_iota(jnp.int32, (tq, ck), 1)
                return (kc == thr) & (spos <= j)
            return count(pred)

        nbits = (nck * ck - 1).bit_length()

        def jstep(b, j):
            cand = j & ~(jnp.int32(1) << (nbits - 1 - b))
            return jnp.where(eq_upto(cand) >= need, cand, j)

        j0 = jnp.full((tq, 1), (1 << nbits) - 1, jnp.int32)
        j = lax.fori_loop(0, nbits, jstep, j0)
        jmax_ref[...] = jnp.broadcast_to(j, jmax_ref.shape)

    jmax = jmax_ref[:, 0:1]
    for c in range(nck):
        @pl.when(c < n_act)
        def _():
            kc = keys_ref[c]
            spos = c * ck + lax.broadcasted_iota(jnp.int32, (tq, ck), 1)
            sel = ((kc > thr) | ((kc == thr) & (spos <= jmax))) & (kc > INT_MIN)
            mask_ref[:, c * ck:(c + 1) * ck] = jnp.where(sel, 1, 0).astype(jnp.int8)

        @pl.when(c >= n_act)
        def _():
            mask_ref[:, c * ck:(c + 1) * ck] = jnp.zeros((tq, ck), jnp.int8)


def index_mask(qi, ki, wi, topk):
    L = qi.shape[0]
    nck = L // IDX_CK
    kit = ki.reshape(nck, IDX_CK, IDX_DIM).transpose(0, 2, 1)
    return pl.pallas_call(
        functools.partial(_index_body, topk=topk, nck=nck),
        grid=(L // IDX_TQ,),
        in_specs=[
            pl.BlockSpec((IDX_TQ, IDX_HEADS * IDX_DIM), lambda i: (i, 0)),
            pl.BlockSpec((nck, IDX_DIM, IDX_CK), lambda i: (0, 0, 0)),
            pl.BlockSpec((IDX_TQ, IDX_HEADS), lambda i: (i, 0)),
        ],
        out_specs=pl.BlockSpec((IDX_TQ, L), lambda i: (i, 0)),
        out_shape=jax.ShapeDtypeStruct((L, L), jnp.int8),
        scratch_shapes=[pltpu.VMEM((nck, IDX_TQ, IDX_CK), jnp.int32), pltpu.VMEM((IDX_TQ, V7X_LANES), jnp.int32)],
        compiler_params=_cparams("parallel"),
        name="index_mask",
    )(qi, kit, wi)


ATT_TQ = 128
ATT_TK = 512


def _attn_body(q_ref, k_ref, v_ref, mask_ref, o_ref, m_ref, l_ref, acc_ref, *, nc):
    i = pl.program_id(0)
    c = pl.program_id(1)
    tq, tk, hd = ATT_TQ, ATT_TK, V7X_LANES
    last = (i * tq + tq - 1) // tk
    rows = Q_PER_KV * tq

    @pl.when(c == 0)
    def _():
        m_ref[...] = jnp.full_like(m_ref, NEG_BIG)
        l_ref[...] = jnp.zeros_like(l_ref)
        acc_ref[...] = jnp.zeros_like(acc_ref)

    @pl.when(c <= last)
    def _():
        bias = jnp.where(mask_ref[...].astype(F32) > 0.5, 0.0, NEG_BIG)
        bias = jnp.concatenate([bias] * Q_PER_KV, axis=0)
        for n in range(N_KV_HEADS):
            qs = jnp.concatenate(
                [q_ref[:, (n * Q_PER_KV + g) * hd:(n * Q_PER_KV + g + 1) * hd] for g in range(Q_PER_KV)], axis=0)
            kn = k_ref[:, n * hd:(n + 1) * hd]
            s = lax.dot_general(qs, kn, (((1,), (1,)), ((), ())), preferred_element_type=F32) + bias
            rs = slice(n * rows, (n + 1) * rows)
            m_prev = m_ref[rs, :]
            m_new = jnp.maximum(m_prev, jnp.max(s, axis=-1, keepdims=True))
            alpha = jnp.exp(m_prev - m_new)
            p = jnp.exp(s - m_new)
            l_ref[rs, :] = alpha * l_ref[rs, :] + jnp.sum(p, axis=-1, keepdims=True)
            acc_ref[rs, :] = alpha * acc_ref[rs, :] + _dot(p.astype(BF16), v_ref[:, n * hd:(n + 1) * hd])
            m_ref[rs, :] = m_new

    @pl.when(c == nc - 1)
    def _():
        for h in range(N_HEADS):
            rs = slice(h * tq, (h + 1) * tq)
            o_ref[:, h * hd:(h + 1) * hd] = (acc_ref[rs, :] / l_ref[rs, :]).astype(o_ref.dtype)


def masked_attention(q, k, v, mask):
    L, dq = q.shape
    dkv = k.shape[1]
    nc = L // ATT_TK
    clamp = lambda i, c: jnp.minimum(c, (i * ATT_TQ + ATT_TQ - 1) // ATT_TK)
    return pl.pallas_call(
        functools.partial(_attn_body, nc=nc),
        grid=(L // ATT_TQ, nc),
        in_specs=[
            pl.BlockSpec((ATT_TQ, dq), lambda i, c: (i, 0)),
            pl.BlockSpec((ATT_TK, dkv), lambda i, c: (clamp(i, c), 0)),
            pl.BlockSpec((ATT_TK, dkv), lambda i, c: (clamp(i, c), 0)),
            pl.BlockSpec((ATT_TQ, ATT_TK), lambda i, c: (i, clamp(i, c))),
        ],
        out_specs=pl.BlockSpec((ATT_TQ, dq), lambda i, c: (i, 0)),
        out_shape=jax.ShapeDtypeStruct((L, dq), BF16),
        scratch_shapes=[pltpu.VMEM((N_HEADS * ATT_TQ, 1), F32), pltpu.VMEM((N_HEADS * ATT_TQ, 1), F32),
                        pltpu.VMEM((N_HEADS * ATT_TQ, V7X_LANES), F32)],
        compiler_params=_cparams("parallel", "arbitrary"),
        name="masked_attention",
    )(q, k, v, mask)


S5_OCT = V7X_SUBLANES
S5_PAD = 8


def _s5_body(res_ref, gin_ref, b_ref, c_ref, ar_ref, ai_ref, d_ref, y_ref, sr_ref, si_ref, hr_ref, hi_ref,
             u_ref, *, tc, nm):
    t_blk = pl.program_id(0)
    pitch = tc + S5_PAD
    uo = u_ref.shape[1] // S5_OCT
    so = nm * V7X_LANES

    @pl.when(t_blk == 0)
    def _():
        hr_ref[...] = jnp.zeros_like(hr_ref)
        hi_ref[...] = jnp.zeros_like(hi_ref)

    u_ref[...] = _rms(res_ref[...], gin_ref[...])
    for s in range(S5_OCT):
        bu = _dot(u_ref[:, s * uo:(s + 1) * uo].astype(BF16), b_ref[s])
        for m in range(nm):
            sr_ref[m, s * pitch:s * pitch + tc, :] = bu[:, m * V7X_LANES:(m + 1) * V7X_LANES]
            si_ref[m, s * pitch:s * pitch + tc, :] = bu[:, so + m * V7X_LANES:so + (m + 1) * V7X_LANES]

    ar = [ar_ref[m] for m in range(nm)]
    ai = [ai_ref[m] for m in range(nm)]

    def step(t, carry):
        hr, hi = carry
        nr, ni = [], []
        for m in range(nm):
            idx = (m, pl.ds(t, S5_OCT, stride=pitch), slice(None))
            r = ar[m] * hr[m] - ai[m] * hi[m] + sr_ref[idx]
            im = ar[m] * hi[m] + ai[m] * hr[m] + si_ref[idx]
            sr_ref[idx] = r
            si_ref[idx] = im
            nr.append(r)
            ni.append(im)
        return tuple(nr), tuple(ni)

    init = (tuple(hr_ref[m] for m in range(nm)), tuple(hi_ref[m] for m in range(nm)))
    hr, hi = lax.fori_loop(0, tc, step, init)
    for m in range(nm):
        hr_ref[m] = hr[m]
        hi_ref[m] = hi[m]

    for s in range(S5_OCT):
        rows = slice(s * pitch, s * pitch + tc)
        h = jnp.concatenate([sr_ref[m, rows, :] for m in range(nm)] + [si_ref[m, rows, :] for m in range(nm)], axis=1)
        cs = slice(s * uo, (s + 1) * uo)
        y = _dot(h.astype(BF16), c_ref[s]) + d_ref[:, cs] * u_ref[:, cs]
        y_ref[:, cs] = jax.nn.gelu(y).astype(y_ref.dtype)


def s5_core(res, gin, lam_re, lam_im, log_dt, b_re, b_im, c_re, c_im, d_skip, *, tc=256):
    L, D = res.shape
    G, P = lam_re.shape
    go = G // S5_OCT
    so = go * P
    nm = so // V7X_LANES
    uo = go * SSM_GROUP
    dt = jnp.exp(log_dt)[:, None]
    mag = jnp.exp(lam_re * dt)
    ar = mag * jnp.cos(lam_im * dt)
    ai = mag * jnp.sin(lam_im * dt)
    den = lam_re * lam_re + lam_im * lam_im
    cr = ((ar - 1.0) * lam_re + ai * lam_im) / den
    ci = (ai * lam_re - (ar - 1.0) * lam_im) / den
    bbr = cr[..., None] * b_re - ci[..., None] * b_im
    bbi = cr[..., None] * b_im + ci[..., None] * b_re
    eye = jnp.eye(go, dtype=F32)

    def blockdiag_in(bb):
        bb = bb.reshape(S5_OCT, go, P, SSM_GROUP)
        return jnp.einsum('sgpc,gh->sgchp', bb, eye).reshape(S5_OCT, uo, so)

    def blockdiag_out(cc):
        cc = cc.reshape(S5_OCT, go, SSM_GROUP, P)
        return jnp.einsum('sgcp,gh->sgphc', cc, eye).reshape(S5_OCT, so, uo)

    bmat = jnp.concatenate([blockdiag_in(bbr), blockdiag_in(bbi)], axis=2).astype(BF16)
    cmat = jnp.concatenate([blockdiag_out(c_re), blockdiag_out(-c_im)], axis=1).astype(BF16)
    a_r = ar.reshape(S5_OCT, nm, V7X_LANES).transpose(1, 0, 2)
    a_i = ai.reshape(S5_OCT, nm, V7X_LANES).transpose(1, 0, 2)

    pitch = tc + S5_PAD
    const3 = lambda t: (0, 0, 0)
    return pl.pallas_call(
        functools.partial(_s5_body, tc=tc, nm=nm),
        grid=(L // tc,),
        in_specs=[
            pl.BlockSpec((tc, D), lambda t: (t, 0)),
            pl.BlockSpec((1, D), lambda t: (0, 0)),
            pl.BlockSpec((S5_OCT, uo, 2 * so), const3, pipeline_mode=pl.Buffered(1)),
            pl.BlockSpec((S5_OCT, 2 * so, uo), const3, pipeline_mode=pl.Buffered(1)),
            pl.BlockSpec((nm, S5_OCT, V7X_LANES), const3),
            pl.BlockSpec((nm, S5_OCT, V7X_LANES), const3),
            pl.BlockSpec((1, D), lambda t: (0, 0)),
        ],
        out_specs=pl.BlockSpec((tc, D), lambda t: (t, 0)),
        out_shape=jax.ShapeDtypeStruct((L, D), BF16),
        scratch_shapes=[
            pltpu.VMEM((nm, S5_OCT * pitch, V7X_LANES), F32),
            pltpu.VMEM((nm, S5_OCT * pitch, V7X_LANES), F32),
            pltpu.VMEM((nm, S5_OCT, V7X_LANES), F32),
            pltpu.VMEM((nm, S5_OCT, V7X_LANES), F32),
            pltpu.VMEM((tc, D), F32),
        ],
        compiler_params=_cparams("arbitrary"),
        name="s5_core",
    )(res, gin.reshape(1, D), bmat, cmat, a_r, a_i, d_skip.reshape(1, D))


def conv_layer(res, g0, g1, w_in, b_in, w_dw, b_dw, ln_g, ln_b, w_out):
    D = res.shape[1]
    w_in = w_in.astype(BF16)
    h = fused_matmul(res, w_in[:, :D], w2=w_in[:, D:], gin=g0, b1=b_in[:D], b2=b_in[D:], name="conv_in_glu")
    h = dwconv_ln_swish(h, w_dw, b_dw, ln_g, ln_b)
    return fused_matmul(h, w_out.astype(BF16), res=res, gout=g1, name="conv_out")


def attn_layer(res, g0, g1, w_in, w_out):
    L, D = res.shape
    dq = D
    dkv = N_KV_HEADS * (D // N_HEADS)
    dqi = IDX_HEADS * IDX_DIM
    width = w_in.shape[1]
    pad = (-width) % V7X_LANES
    w_in = jnp.pad(w_in.astype(BF16), ((0, 0), (0, pad)))
    proj = fused_matmul(res, w_in, gin=g0, tn=_pick_tn(width + pad, 512), name="attn_in")
    q, k, v, qi, ki, wi = rope_split(proj, dq, dkv, dqi)
    mask = index_mask(qi, ki, wi, min(TOPK_MAX, L // 4))
    o = masked_attention(q, k, v, mask)
    return fused_matmul(o, w_out.astype(BF16), res=res, gout=g1, name="attn_out")


def s5_layer(res, g0, g1, lam_re, lam_im, log_dt, b_re, b_im, c_re, c_im, d_skip, w_glu):
    D = res.shape[1]
    y = s5_core(res, g0, lam_re, lam_im, log_dt, b_re, b_im, c_re, c_im, d_skip)
    w_glu = w_glu.astype(BF16)
    return fused_matmul(y, w_glu[:, :D], w2=w_glu[:, D:], res=res, gout=g1, name="s5_glu")


def kernel(x, norm_gains, mlp_w_up, mlp_w_down, conv_w_in, conv_b_in, conv_w_dw, conv_b_dw, conv_ln_g, conv_ln_b, conv_w_out, pool_w, pool_scale, attn_w_in, attn_w_out, ssm_lambda_re, ssm_lambda_im, ssm_log_dt, ssm_b_re, ssm_b_im, ssm_c_re, ssm_c_im, ssm_d, ssm_w_glu):
    B, L, D = x.shape
    depth = norm_gains.shape[0]
    outs = []
    for b in range(B):
        res = x[b]
        for i in range(depth):
            m, j = i % 4, i // 4
            g = norm_gains[i]
            if m == 0:
                res = conv_layer(res, g[0], g[1], conv_w_in[j], conv_b_in[j], conv_w_dw[j], conv_b_dw[j],
                                 conv_ln_g[j], conv_ln_b[j], conv_w_out[j])
            elif m == 1:
                res = pool_layer(res, g[0], pool_w[j].astype(BF16), pool_scale[j], g[1])
            elif m == 2:
                res = attn_layer(res, g[0], g[1], attn_w_in[j], attn_w_out[j])
            else:
                res = s5_layer(res, g[0], g[1], ssm_lambda_re[j], ssm_lambda_im[j], ssm_log_dt[j], ssm_b_re[j],
                               ssm_b_im[j], ssm_c_re[j], ssm_c_im[j], ssm_d[j], ssm_w_glu[j])
            res = mlp_layer(res, g[2], mlp_w_up[i].astype(BF16), mlp_w_down[i].astype(BF16), g[3])
        outs.append(res)
    return jnp.stack(outs, axis=0)
```

```python
import functools
import math

import jax
import jax.numpy as jnp
from jax import lax
from jax.experimental import pallas as pl
from jax.experimental.pallas import tpu as pltpu

NORM_EPS = 1e-6
LN_EPS = 1e-5
CONV_WIDTH = 31
POOL_WINDOWS = (2, 4, 8, 16)
N_HEADS = 16
N_KV_HEADS = 4
Q_PER_KV = N_HEADS // N_KV_HEADS
IDX_HEADS = 16
IDX_DIM = 64
TOPK_MAX = 256
ROPE_THETA = 500000.0
SSM_GROUP = 16
SSM_STATE = 64

V7X_LANES = 128
V7X_SUBLANES = 8
V7X_VMEM_BYTES = 64 * 1024 * 1024
VMEM_LIMIT = V7X_VMEM_BYTES - 8 * 1024 * 1024

INT_MIN = -(2 ** 31)
NEG_BIG = -1e30
F32 = jnp.float32
BF16 = jnp.bfloat16


def _cparams(*sem):
    return pltpu.CompilerParams(dimension_semantics=sem, vmem_limit_bytes=VMEM_LIMIT)


def _rms(x, g):
    ms = jnp.mean(x * x, axis=-1, keepdims=True)
    return x * lax.rsqrt(ms + NORM_EPS) * g


def _pick_tn(n, cap):
    return max(t for t in range(V7X_LANES, cap + 1, V7X_LANES) if n % t == 0)


def _dot(a, b):
    return jnp.dot(a, b, preferred_element_type=F32)


def _fm_body(*refs, has_norm, has_bias, glu, has_epi, nj, tn):
    it = iter(refs)
    x_ref = next(it)
    gin_ref = next(it) if has_norm else None
    w1_ref = next(it)
    w2_ref = next(it) if glu else None
    b1_ref = next(it) if has_bias else None
    b2_ref = next(it) if (has_bias and glu) else None
    res_ref = next(it) if has_epi else None
    gout_ref = next(it) if has_epi else None
    out_ref = next(it)
    xn_ref = next(it)
    acc_ref = next(it) if has_epi else None

    j = pl.program_id(1)

    @pl.when(j == 0)
    def _():
        x = x_ref[...].astype(F32)
        if has_norm:
            x = _rms(x, gin_ref[...])
        xn_ref[...] = x.astype(BF16)

    xn = xn_ref[...]
    y = _dot(xn, w1_ref[...])
    if has_bias:
        y = y + b1_ref[...]
    if glu:
        y2 = _dot(xn, w2_ref[...])
        if has_bias:
            y2 = y2 + b2_ref[...]
        y = y * jax.nn.sigmoid(y2)
    if not has_epi:
        out_ref[...] = y.astype(out_ref.dtype)
    else:
        acc_ref[j] = y

        @pl.when(j == nj - 1)
        def _():
            ssq = jnp.zeros((acc_ref.shape[1], 1), F32)
            for jj in range(nj):
                a = acc_ref[jj]
                ssq = ssq + jnp.sum(a * a, axis=-1, keepdims=True)
            rs = lax.rsqrt(ssq / (nj * tn) + NORM_EPS)
            for jj in range(nj):
                sl = slice(jj * tn, (jj + 1) * tn)
                out_ref[:, sl] = res_ref[:, sl] + acc_ref[jj] * rs * gout_ref[:, sl]


def fused_matmul(x, w, *, glu=False, gin=None, b=None, res=None, gout=None,
                 out_dtype=F32, tm=512, tn=512, name="fused_matmul"):
    L, K = x.shape
    N = w.shape[1] // 2 if glu else w.shape[1]
    tn = min(tn, N)
    assert L % tm == 0 and N % tn == 0
    nj = N // tn
    has_norm, has_bias, has_epi = gin is not None, b is not None, res is not None

    args = [x]
    specs = [pl.BlockSpec((tm, K), lambda i, j: (i, 0))]
    if has_norm:
        args.append(gin.reshape(1, K).astype(F32))
        specs.append(pl.BlockSpec((1, K), lambda i, j: (0, 0)))
    args.append(w)
    specs.append(pl.BlockSpec((K, tn), lambda i, j: (0, j)))
    if glu:
        args.append(w)
        specs.append(pl.BlockSpec((K, tn), lambda i, j: (0, j + nj)))
    if has_bias:
        b = b.reshape(1, -1).astype(F32)
        args.append(b)
        specs.append(pl.BlockSpec((1, tn), lambda i, j: (0, j)))
        if glu:
            args.append(b)
            specs.append(pl.BlockSpec((1, tn), lambda i, j: (0, j + nj)))
    scratch = [pltpu.VMEM((tm, K), BF16)]
    if has_epi:
        args += [res, gout.reshape(1, N).astype(F32)]
        specs += [pl.BlockSpec((tm, N), lambda i, j: (i, 0)), pl.BlockSpec((1, N), lambda i, j: (0, 0))]
        out_spec = pl.BlockSpec((tm, N), lambda i, j: (i, 0))
        out_dtype = F32
        scratch.append(pltpu.VMEM((nj, tm, tn), F32))
    else:
        out_spec = pl.BlockSpec((tm, tn), lambda i, j: (i, j))

    body = functools.partial(_fm_body, has_norm=has_norm, has_bias=has_bias, glu=glu,
                             has_epi=has_epi, nj=nj, tn=tn)
    return pl.pallas_call(
        body,
        grid=(L // tm, nj),
        in_specs=specs,
        out_specs=out_spec,
        out_shape=jax.ShapeDtypeStruct((L, N), out_dtype),
        scratch_shapes=scratch,
        compiler_params=_cparams("parallel", "arbitrary"),
        name=name,
    )(*args)


def _mlp_body(res_ref, gin_ref, wu_ref, wd_ref, gout_ref, out_ref, xn_ref, acc_ref, *, nj):
    j = pl.program_id(1)

    @pl.when(j == 0)
    def _():
        xn_ref[...] = _rms(res_ref[...], gin_ref[...]).astype(BF16)
        acc_ref[...] = jnp.zeros_like(acc_ref)

    h = _dot(xn_ref[...], wu_ref[...])
    h = jnp.maximum(h, 0.0)
    acc_ref[...] += _dot((h * h).astype(BF16), wd_ref[...])

    @pl.when(j == nj - 1)
    def _():
        out_ref[...] = res_ref[...] + _rms(acc_ref[...], gout_ref[...])


def mlp_layer(res, gin, w_up, w_down, gout, *, tm=512, tf=512):
    L, D = res.shape
    F = w_up.shape[1]
    nj = F // tf
    return pl.pallas_call(
        functools.partial(_mlp_body, nj=nj),
        grid=(L // tm, nj),
        in_specs=[
            pl.BlockSpec((tm, D), lambda i, j: (i, 0)),
            pl.BlockSpec((1, D), lambda i, j: (0, 0)),
            pl.BlockSpec((D, tf), lambda i, j: (0, j)),
            pl.BlockSpec((tf, D), lambda i, j: (j, 0)),
            pl.BlockSpec((1, D), lambda i, j: (0, 0)),
        ],
        out_specs=pl.BlockSpec((tm, D), lambda i, j: (i, 0)),
        out_shape=jax.ShapeDtypeStruct((L, D), F32),
        scratch_shapes=[pltpu.VMEM((tm, D), BF16), pltpu.VMEM((tm, D), F32)],
        compiler_params=_cparams("parallel", "arbitrary"),
        name="mlp",
    )(res, gin.reshape(1, D), w_up, w_down, gout.reshape(1, D))


CONV_HALO = 32
CONV_ROWS = 32


def _dwconv_body(cur_ref, halo_ref, w_ref, b_ref, lg_ref, lb_ref, out_ref, sh_ref, cv_ref, *, tm):
    i = pl.program_id(0)
    D = cur_ref.shape[1]
    sub, lanes = V7X_SUBLANES, V7X_LANES
    ncol = D // lanes
    row_chunks = range(0, tm, CONV_ROWS)
    halo = halo_ref[...]
    halo = jnp.where(i == 0, jnp.zeros_like(halo), halo)
    for c in range(ncol):
        sh_ref[c, 0, 0:CONV_HALO, :] = halo[:, c * lanes:(c + 1) * lanes]
        sh_ref[c, 0, CONV_HALO:, :] = cur_ref[:, c * lanes:(c + 1) * lanes]
    n = tm + CONV_HALO - sub

    def column(c, carry):
        for j in range(1, sub):
            sh_ref[c, j, sub:, :] = sh_ref[c, 0, sub - j:sub - j + n, :]
        accs = [jnp.zeros((CONV_ROWS, lanes), F32) + b_ref[c] for _ in row_chunks]
        for k in range(CONV_WIDTH):
            a, j = divmod(CONV_WIDTH - 1 - k, sub)
            wk = jnp.broadcast_to(w_ref[c, k:k + 1, :], (CONV_ROWS, lanes))
            for ri, r0 in enumerate(row_chunks):
                start = CONV_HALO + r0 - sub * a
                accs[ri] = accs[ri] + sh_ref[c, j, start:start + CONV_ROWS, :] * wk
        for ri, r0 in enumerate(row_chunks):
            cv_ref[c, r0:r0 + CONV_ROWS, :] = accs[ri]
        return carry

    lax.fori_loop(0, ncol, column, 0)

    s1 = cv_ref[0]
    for c in range(1, ncol):
        s1 = s1 + cv_ref[c]
    mu = jnp.sum(s1, axis=-1, keepdims=True) / D
    s2 = jnp.zeros((tm, lanes), F32)
    for c in range(ncol):
        hc = cv_ref[c] - mu
        s2 = s2 + hc * hc
    rstd = lax.rsqrt(jnp.sum(s2, axis=-1, keepdims=True) / D + LN_EPS)
    for c in range(ncol):
        cs = slice(c * lanes, (c + 1) * lanes)
        y = (cv_ref[c] - mu) * rstd * lg_ref[:, cs] + lb_ref[:, cs]
        out_ref[:, cs] = (y * jax.nn.sigmoid(y)).astype(out_ref.dtype)


def dwconv_ln_swish(h, w_dw, b_dw, ln_g, ln_b, *, tm=256):
    L, D = h.shape
    hb = tm // CONV_HALO
    ncol = D // V7X_LANES
    w_cols = w_dw.reshape(CONV_WIDTH, ncol, V7X_LANES).transpose(1, 0, 2)
    b_cols = b_dw.reshape(ncol, 1, V7X_LANES)
    return pl.pallas_call(
        functools.partial(_dwconv_body, tm=tm),
        grid=(L // tm,),
        in_specs=[
            pl.BlockSpec((tm, D), lambda i: (i, 0)),
            pl.BlockSpec((CONV_HALO, D), lambda i: (jnp.maximum(i * hb - 1, 0), 0)),
            pl.BlockSpec((ncol, CONV_WIDTH, V7X_LANES), lambda i: (0, 0, 0)),
            pl.BlockSpec((ncol, 1, V7X_LANES), lambda i: (0, 0, 0)),
            pl.BlockSpec((1, D), lambda i: (0, 0)),
            pl.BlockSpec((1, D), lambda i: (0, 0)),
        ],
        out_specs=pl.BlockSpec((tm, D), lambda i: (i, 0)),
        out_shape=jax.ShapeDtypeStruct((L, D), BF16),
        scratch_shapes=[pltpu.VMEM((ncol, V7X_SUBLANES, tm + CONV_HALO, V7X_LANES), F32),
                        pltpu.VMEM((ncol, tm, V7X_LANES), F32)],
        compiler_params=_cparams("parallel"),
        name="dwconv_ln_swish",
    )(h, h, w_cols, b_cols, ln_g.reshape(1, D), ln_b.reshape(1, D))


POOL_HALO = 16


def _pool_body(cur_ref, halo_ref, gin_ref, pw_ref, ps_ref, gout_ref, out_ref, ext_ref, mix_ref, *, tm):
    i = pl.program_id(0)
    D = cur_ref.shape[1]
    cg = D // len(POOL_WINDOWS)
    hn = _rms(halo_ref[...], gin_ref[...])
    ext_ref[0:POOL_HALO, :] = jnp.where(i == 0, jnp.zeros_like(hn), hn)
    ext_ref[POOL_HALO:, :] = _rms(cur_ref[...], gin_ref[...])
    t1 = (i * tm + 1 + lax.broadcasted_iota(jnp.int32, (tm, 1), 0)).astype(F32)
    for g, w in enumerate(POOL_WINDOWS):
        cs = slice(g * cg, (g + 1) * cg)
        s = ext_ref[POOL_HALO:POOL_HALO + tm, cs]
        x = s
        for d in range(1, w):
            s = s + ext_ref[POOL_HALO - d:POOL_HALO - d + tm, cs]
        pooled = s / jnp.minimum(t1, float(w)) - x
        mix_ref[:, cs] = _dot(pooled.astype(BF16), pw_ref[g]) * ps_ref[:, cs]
    out_ref[...] = cur_ref[...] + _rms(mix_ref[...], gout_ref[...])


def pool_layer(res, gin, pool_w, pool_scale, gout, *, tm=256):
    L, D = res.shape
    ng = len(POOL_WINDOWS)
    cg = D // ng
    hb = tm // POOL_HALO
    return pl.pallas_call(
        functools.partial(_pool_body, tm=tm),
        grid=(L // tm,),
        in_specs=[
            pl.BlockSpec((tm, D), lambda i: (i, 0)),
            pl.BlockSpec((POOL_HALO, D), lambda i: (jnp.maximum(i * hb - 1, 0), 0)),
            pl.BlockSpec((1, D), lambda i: (0, 0)),
            pl.BlockSpec((ng, cg, cg), lambda i: (0, 0, 0)),
            pl.BlockSpec((1, D), lambda i: (0, 0)),
            pl.BlockSpec((1, D), lambda i: (0, 0)),
        ],
        out_specs=pl.BlockSpec((tm, D), lambda i: (i, 0)),
        out_shape=jax.ShapeDtypeStruct((L, D), F32),
        scratch_shapes=[pltpu.VMEM((tm + POOL_HALO, D), F32), pltpu.VMEM((tm, D), F32)],
        compiler_params=_cparams("parallel"),
        name="pool_layer",
    )(res, res, gin.reshape(1, D), pool_w, pool_scale.reshape(1, D), gout.reshape(1, D))


def _rope_tables(L, width, r):
    half = r // 2
    inv = jnp.power(ROPE_THETA, -2.0 * jnp.arange(half, dtype=F32) / r)
    ang = jnp.arange(L, dtype=F32)[:, None] * inv[None, :]
    cos, sin = jnp.cos(ang), jnp.sin(ang)
    pad = width - r
    cos_h = jnp.concatenate([cos, cos, jnp.ones((L, pad), F32)], axis=1)
    sin_h = jnp.concatenate([-sin, sin, jnp.zeros((L, pad), F32)], axis=1)
    reps = V7X_LANES // width
    return jnp.tile(cos_h, (1, reps)), jnp.tile(sin_h, (1, reps))


def _rope128(x, cos_t, sin_t, half, width):
    lane = lax.broadcasted_iota(jnp.int32, x.shape, 1) % width
    fwd = pltpu.roll(x, V7X_LANES - half, axis=1)
    bwd = pltpu.roll(x, half, axis=1)
    partner = jnp.where(lane < half, fwd, bwd)
    return x * cos_t + partner * sin_t


def _rope_body(p_ref, cq_ref, sq_ref, ci_ref, si_ref, q_ref, k_ref, v_ref, qi_ref, ki_ref, wi_ref,
               *, dq, dkv, dqi, q_scale, wi_scale):
    cq, sq, ci, si = cq_ref[...], sq_ref[...], ci_ref[...], si_ref[...]
    hd = V7X_LANES
    for h in range(dq // hd):
        x = p_ref[:, h * hd:(h + 1) * hd]
        q_ref[:, h * hd:(h + 1) * hd] = (_rope128(x, cq, sq, 16, hd) * q_scale).astype(BF16)
    o = dq
    for h in range(dkv // hd):
        x = p_ref[:, o + h * hd:o + (h + 1) * hd]
        k_ref[:, h * hd:(h + 1) * hd] = _rope128(x, cq, sq, 16, hd).astype(BF16)
    o += dkv
    v_ref[...] = p_ref[:, o:o + dkv].astype(BF16)
    o += dkv
    for h in range(dqi // hd):
        x = p_ref[:, o + h * hd:o + (h + 1) * hd]
        qi_ref[:, h * hd:(h + 1) * hd] = _rope128(x, ci, si, 8, IDX_DIM).astype(BF16)
    o += dqi
    x = p_ref[:, o:o + hd]
    xr = _rope128(x, ci, si, 8, IDX_DIM)
    ki_ref[...] = xr[:, :IDX_DIM].astype(BF16)
    wi_ref[...] = x[:, IDX_DIM:IDX_DIM + IDX_HEADS] * wi_scale


def rope_split(proj, dq, dkv, dqi, *, tm=256):
    L, W = proj.shape
    cq, sq = _rope_tables(L, V7X_LANES, 32)
    ci, si = _rope_tables(L, IDX_DIM, 16)
    tab = pl.BlockSpec((tm, V7X_LANES), lambda i: (i, 0))
    row = lambda n: pl.BlockSpec((tm, n), lambda i: (i, 0))
    body = functools.partial(_rope_body, dq=dq, dkv=dkv, dqi=dqi, q_scale=V7X_LANES ** -0.5 * LOG2E,
                             wi_scale=IDX_HEADS ** -0.5 * IDX_DIM ** -0.5)
    return pl.pallas_call(
        body,
        grid=(L // tm,),
        in_specs=[row(W), tab, tab, tab, tab],
        out_specs=[row(dq), row(dkv), row(dkv), row(dqi), row(IDX_DIM), row(IDX_HEADS)],
        out_shape=[
            jax.ShapeDtypeStruct((L, dq), BF16), jax.ShapeDtypeStruct((L, dkv), BF16),
            jax.ShapeDtypeStruct((L, dkv), BF16), jax.ShapeDtypeStruct((L, dqi), BF16),
            jax.ShapeDtypeStruct((L, IDX_DIM), BF16), jax.ShapeDtypeStruct((L, IDX_HEADS), F32),
        ],
        compiler_params=_cparams("parallel"),
        name="rope_split",
    )(proj, cq, sq, ci, si)


IDX_TQ = 128
IDX_CK = 512


def _sortable(x):
    b = pltpu.bitcast(x + 0.0, jnp.int32)
    return jnp.where(b < 0, b ^ jnp.int32(0x7FFFFFFF), b)


def _index_body(qi_ref, kit_ref, wi_ref, mask_ref, keys_ref, jmax_ref, qs_ref, *, topk, nck):
    i = pl.program_id(0)
    tq, ck = IDX_TQ, IDX_CK
    n_act = (i * tq + tq + ck - 1) // ck
    tpos = i * tq + lax.broadcasted_iota(jnp.int32, (tq, 1), 0)
    wi = wi_ref[...]
    for h in range(IDX_HEADS):
        qs_ref[h * tq:(h + 1) * tq, :] = qi_ref[:, h * IDX_DIM:(h + 1) * IDX_DIM]

    def fill(c, carry):
        sc = _dot(qs_ref[...], kit_ref[c])
        acc = jnp.zeros((tq, ck), F32)
        for h in range(IDX_HEADS):
            acc = acc + jnp.maximum(sc[h * tq:(h + 1) * tq], 0.0) * wi[:, h:h + 1]
        spos = c * ck + lax.broadcasted_iota(jnp.int32, (tq, ck), 1)
        keys_ref[c] = jnp.where(spos <= tpos, _sortable(acc), INT_MIN)
        return carry

    lax.fori_loop(0, n_act, fill, 0)

    def count(pred):
        def step(c, part):
            m = jnp.where(pred(keys_ref[c], c), 1, 0)
            for l0 in range(0, ck, V7X_LANES):
                part = part + m[:, l0:l0 + V7X_LANES]
            return part
        part = lax.fori_loop(0, n_act, step, jnp.zeros((tq, V7X_LANES), jnp.int32))
        return jnp.sum(part, axis=-1, keepdims=True)

    thr = jnp.where(count(lambda kc, c: kc >= 0) >= topk, jnp.int32(0), jnp.int32(INT_MIN))

    def bit_step(b, thr):
        cand = thr | (jnp.int32(1) << (30 - b))
        return jnp.where(count(lambda kc, c: kc >= cand) >= topk, cand, thr)

    thr = lax.fori_loop(0, 31, bit_step, thr)

    need = topk - count(lambda kc, c: kc > thr)
    n_eq = count(lambda kc, c: kc == thr)
    jmax_ref[...] = jnp.full(jmax_ref.shape, nck * ck, jnp.int32)

    @pl.when(jnp.max(n_eq - need) > 0)
    def _():
        def eq_upto(j):
            def pred(kc, c):
                spos = c * ck + lax.broadcasted_iota(jnp.int32, (tq, ck), 1)
                return (kc == thr) & (spos <= j)
            return count(pred)

        nbits = (nck * ck - 1).bit_length()

        def jstep(b, j):
            cand = j & ~(jnp.int32(1) << (nbits - 1 - b))
            return jnp.where(eq_upto(cand) >= need, cand, j)

        j0 = jnp.full((tq, 1), (1 << nbits) - 1, jnp.int32)
        j = lax.fori_loop(0, nbits, jstep, j0)
        jmax_ref[...] = jnp.broadcast_to(j, jmax_ref.shape)

    jmax = jmax_ref[:, 0:1]
    for c in range(nck):
        @pl.when(c < n_act)
        def _():
            kc = keys_ref[c]
            spos = c * ck + lax.broadcasted_iota(jnp.int32, (tq, ck), 1)
            sel = ((kc > thr) | ((kc == thr) & (spos <= jmax))) & (kc > INT_MIN)
            mask_ref[:, c * ck:(c + 1) * ck] = jnp.where(sel, 1, 0).astype(jnp.int8)

        @pl.when(c >= n_act)
        def _():
            mask_ref[:, c * ck:(c + 1) * ck] = jnp.zeros((tq, ck), jnp.int8)


def index_mask(qi, ki, wi, topk):
    L = qi.shape[0]
    nck = L // IDX_CK
    kit = ki.reshape(nck, IDX_CK, IDX_DIM).transpose(0, 2, 1)
    return pl.pallas_call(
        functools.partial(_index_body, topk=topk, nck=nck),
        grid=(L // IDX_TQ,),
        in_specs=[
            pl.BlockSpec((IDX_TQ, IDX_HEADS * IDX_DIM), lambda i: (i, 0)),
            pl.BlockSpec((nck, IDX_DIM, IDX_CK), lambda i: (0, 0, 0)),
            pl.BlockSpec((IDX_TQ, IDX_HEADS), lambda i: (i, 0)),
        ],
        out_specs=pl.BlockSpec((IDX_TQ, L), lambda i: (i, 0)),
        out_shape=jax.ShapeDtypeStruct((L, L), jnp.int8),
        scratch_shapes=[pltpu.VMEM((nck, IDX_TQ, IDX_CK), jnp.int32), pltpu.VMEM((IDX_TQ, V7X_LANES), jnp.int32),
                        pltpu.VMEM((IDX_HEADS * IDX_TQ, IDX_DIM), BF16)],
        compiler_params=_cparams("parallel"),
        name="index_mask",
    )(qi, kit, wi)


ATT_TQ = 256
ATT_TK = 512
ATT_RB = 32
LOG2E = 1.4426950408889634


def _attn_body(q_ref, k_ref, v_ref, mask_ref, o_ref, m_ref, l_ref, acc_ref, s_ref, p_ref, a_ref, bias_ref, *, nc):
    i = pl.program_id(0)
    c = pl.program_id(1)
    tq, tk, hd, rb = ATT_TQ, ATT_TK, V7X_LANES, ATT_RB
    last = (i * tq + tq - 1) // tk
    rows = Q_PER_KV * tq
    nlc = tk // hd

    @pl.when(c == 0)
    def _():
        m_ref[...] = jnp.full_like(m_ref, NEG_BIG)
        l_ref[...] = jnp.zeros_like(l_ref)
        acc_ref[...] = jnp.zeros_like(acc_ref)

    @pl.when(c <= last)
    def _():
        bias_ref[...] = jnp.where(mask_ref[...].astype(F32) > 0.5, 0.0, NEG_BIG)
        for n in range(N_KV_HEADS):
            qs = jnp.concatenate(
                [q_ref[:, (n * Q_PER_KV + g) * hd:(n * Q_PER_KV + g + 1) * hd] for g in range(Q_PER_KV)], axis=0)
            kn = k_ref[:, n * hd:(n + 1) * hd]
            s = lax.dot_general(qs, kn, (((1,), (1,)), ((), ())), preferred_element_type=F32)
            s_ref[...] = (s.reshape(Q_PER_KV, tq, tk) + bias_ref[...][None]).reshape(rows, tk)

            def max_rows(r, carry):
                r0 = pl.multiple_of(r * rb, rb)
                hr = pl.ds(n * rows + r0, rb)
                x = s_ref[pl.ds(r0, rb), :]
                mx = x[:, 0:hd]
                for j in range(1, nlc):
                    mx = jnp.maximum(mx, x[:, j * hd:(j + 1) * hd])
                m_prev = m_ref[hr, :]
                m_new = jnp.maximum(m_prev, jnp.max(mx, axis=-1, keepdims=True))
                a_ref[pl.ds(r0, rb), :] = jnp.exp2(m_prev - m_new)
                m_ref[hr, :] = m_new
                return carry

            lax.fori_loop(0, rows // rb, max_rows, 0, unroll=True)

            def exp_rows(r, carry):
                r0 = pl.multiple_of(r * rb, rb)
                hr = pl.ds(n * rows + r0, rb)
                m_new = m_ref[hr, :]
                psum = jnp.zeros((rb, hd), F32)
                for j in range(nlc):
                    p = jnp.exp2(s_ref[pl.ds(r0, rb), j * hd:(j + 1) * hd] - m_new)
                    psum = psum + p
                    p_ref[pl.ds(r0, rb), j * hd:(j + 1) * hd] = p.astype(BF16)
                l_ref[hr, :] = a_ref[pl.ds(r0, rb), :] * l_ref[hr, :] + psum
                return carry

            lax.fori_loop(0, rows // rb, exp_rows, 0, unroll=4)
            rs = slice(n * rows, (n + 1) * rows)
            acc_ref[rs, :] = a_ref[...] * acc_ref[rs, :] + _dot(p_ref[...], v_ref[:, n * hd:(n + 1) * hd])

    @pl.when(c == nc - 1)
    def _():
        for h in range(N_HEADS):
            rs = slice(h * tq, (h + 1) * tq)
            l = jnp.sum(l_ref[rs, :], axis=-1, keepdims=True)
            o_ref[:, h * hd:(h + 1) * hd] = (acc_ref[rs, :] / l).astype(o_ref.dtype)


def masked_attention(q, k, v, mask):
    L, dq = q.shape
    dkv = k.shape[1]
    tq = ATT_TQ
    nc = L // ATT_TK
    clamp = lambda i, c: jnp.minimum(c, (i * tq + tq - 1) // ATT_TK)
    stat = pltpu.VMEM((N_HEADS * tq, V7X_LANES), F32)
    return pl.pallas_call(
        functools.partial(_attn_body, nc=nc),
        grid=(L // tq, nc),
        in_specs=[
            pl.BlockSpec((tq, dq), lambda i, c: (i, 0)),
            pl.BlockSpec((ATT_TK, dkv), lambda i, c: (clamp(i, c), 0)),
            pl.BlockSpec((ATT_TK, dkv), lambda i, c: (clamp(i, c), 0)),
            pl.BlockSpec((tq, ATT_TK), lambda i, c: (i, clamp(i, c))),
        ],
        out_specs=pl.BlockSpec((tq, dq), lambda i, c: (i, 0)),
        out_shape=jax.ShapeDtypeStruct((L, dq), BF16),
        scratch_shapes=[stat, stat, stat,
                        pltpu.VMEM((Q_PER_KV * tq, ATT_TK), F32), pltpu.VMEM((Q_PER_KV * tq, ATT_TK), BF16),
                        pltpu.VMEM((Q_PER_KV * tq, V7X_LANES), F32), pltpu.VMEM((tq, ATT_TK), F32)],
        compiler_params=_cparams("parallel", "arbitrary"),
        name="masked_attention",
    )(q, k, v, mask)


S5_OCT = V7X_SUBLANES
S5_PAD = 8


def _s5_body(res_ref, gin_ref, b_ref, c_ref, ar_ref, ai_ref, d_ref, y_ref, sr_ref, si_ref, hr_ref, hi_ref,
             u_ref, *, tc, nm):
    t_blk = pl.program_id(0)
    pitch = tc + S5_PAD
    uo = u_ref.shape[1] // S5_OCT
    so = nm * V7X_LANES

    @pl.when(t_blk == 0)
    def _():
        hr_ref[...] = jnp.zeros_like(hr_ref)
        hi_ref[...] = jnp.zeros_like(hi_ref)

    u_ref[...] = _rms(res_ref[...], gin_ref[...])
    for s in range(S5_OCT):
        bu = _dot(u_ref[:, s * uo:(s + 1) * uo].astype(BF16), b_ref[s])
        for m in range(nm):
            sr_ref[m, s * pitch:s * pitch + tc, :] = bu[:, m * V7X_LANES:(m + 1) * V7X_LANES]
            si_ref[m, s * pitch:s * pitch + tc, :] = bu[:, so + m * V7X_LANES:so + (m + 1) * V7X_LANES]

    ar = [ar_ref[m] for m in range(nm)]
    ai = [ai_ref[m] for m in range(nm)]

    def step(t, carry):
        hr, hi = carry
        nr, ni = [], []
        for m in range(nm):
            idx = (m, pl.ds(t, S5_OCT, stride=pitch), slice(None))
            r = ar[m] * hr[m] - ai[m] * hi[m] + sr_ref[idx]
            im = ar[m] * hi[m] + ai[m] * hr[m] + si_ref[idx]
            sr_ref[idx] = r
            si_ref[idx] = im
            nr.append(r)
            ni.append(im)
        return tuple(nr), tuple(ni)

    init = (tuple(hr_ref[m] for m in range(nm)), tuple(hi_ref[m] for m in range(nm)))
    hr, hi = lax.fori_loop(0, tc, step, init)
    for m in range(nm):
        hr_ref[m] = hr[m]
        hi_ref[m] = hi[m]

    for s in range(S5_OCT):
        rows = slice(s * pitch, s * pitch + tc)
        h = jnp.concatenate([sr_ref[m, rows, :] for m in range(nm)] + [si_ref[m, rows, :] for m in range(nm)], axis=1)
        cs = slice(s * uo, (s + 1) * uo)
        y = _dot(h.astype(BF16), c_ref[s]) + d_ref[:, cs] * u_ref[:, cs]
        y_ref[:, cs] = jax.nn.gelu(y).astype(y_ref.dtype)


def s5_core(res, gin, lam_re, lam_im, log_dt, b_re, b_im, c_re, c_im, d_skip, *, tc=256):
    L, D = res.shape
    G, P = lam_re.shape
    go = G // S5_OCT
    so = go * P
    nm = so // V7X_LANES
    uo = go * SSM_GROUP
    dt = jnp.exp(log_dt)[:, None]
    mag = jnp.exp(lam_re * dt)
    ar = mag * jnp.cos(lam_im * dt)
    ai = mag * jnp.sin(lam_im * dt)
    den = lam_re * lam_re + lam_im * lam_im
    cr = ((ar - 1.0) * lam_re + ai * lam_im) / den
    ci = (ai * lam_re - (ar - 1.0) * lam_im) / den
    bbr = cr[..., None] * b_re - ci[..., None] * b_im
    bbi = cr[..., None] * b_im + ci[..., None] * b_re
    eye = jnp.eye(go, dtype=F32)

    def blockdiag_in(bb):
        bb = bb.reshape(S5_OCT, go, P, SSM_GROUP)
        return jnp.einsum('sgpc,gh->sgchp', bb, eye).reshape(S5_OCT, uo, so)

    def blockdiag_out(cc):
        cc = cc.reshape(S5_OCT, go, SSM_GROUP, P)
        return jnp.einsum('sgcp,gh->sgphc', cc, eye).reshape(S5_OCT, so, uo)

    bmat = jnp.concatenate([blockdiag_in(bbr), blockdiag_in(bbi)], axis=2).astype(BF16)
    cmat = jnp.concatenate([blockdiag_out(c_re), blockdiag_out(-c_im)], axis=1).astype(BF16)
    a_r = ar.reshape(S5_OCT, nm, V7X_LANES).transpose(1, 0, 2)
    a_i = ai.reshape(S5_OCT, nm, V7X_LANES).transpose(1, 0, 2)

    pitch = tc + S5_PAD
    const3 = lambda t: (0, 0, 0)
    return pl.pallas_call(
        functools.partial(_s5_body, tc=tc, nm=nm),
        grid=(L // tc,),
        in_specs=[
            pl.BlockSpec((tc, D), lambda t: (t, 0)),
            pl.BlockSpec((1, D), lambda t: (0, 0)),
            pl.BlockSpec((S5_OCT, uo, 2 * so), const3, pipeline_mode=pl.Buffered(1)),
            pl.BlockSpec((S5_OCT, 2 * so, uo), const3, pipeline_mode=pl.Buffered(1)),
            pl.BlockSpec((nm, S5_OCT, V7X_LANES), const3),
            pl.BlockSpec((nm, S5_OCT, V7X_LANES), const3),
            pl.BlockSpec((1, D), lambda t: (0, 0)),
        ],
        out_specs=pl.BlockSpec((tc, D), lambda t: (t, 0)),
        out_shape=jax.ShapeDtypeStruct((L, D), BF16),
        scratch_shapes=[
            pltpu.VMEM((nm, S5_OCT * pitch, V7X_LANES), F32),
            pltpu.VMEM((nm, S5_OCT * pitch, V7X_LANES), F32),
            pltpu.VMEM((nm, S5_OCT, V7X_LANES), F32),
            pltpu.VMEM((nm, S5_OCT, V7X_LANES), F32),
            pltpu.VMEM((tc, D), F32),
        ],
        compiler_params=_cparams("arbitrary"),
        name="s5_core",
    )(res, gin.reshape(1, D), bmat, cmat, a_r, a_i, d_skip.reshape(1, D))


def conv_layer(res, g0, g1, w_in, b_in, w_dw, b_dw, ln_g, ln_b, w_out):
    h = fused_matmul(res, w_in.astype(BF16), glu=True, gin=g0, b=b_in, name="conv_in_glu")
    h = dwconv_ln_swish(h, w_dw, b_dw, ln_g, ln_b)
    return fused_matmul(h, w_out.astype(BF16), res=res, gout=g1, name="conv_out")


def attn_layer(res, g0, g1, w_in, w_out):
    L, D = res.shape
    dq = D
    dkv = N_KV_HEADS * (D // N_HEADS)
    dqi = IDX_HEADS * IDX_DIM
    width = w_in.shape[1]
    pad = (-width) % V7X_LANES
    w_in = jnp.pad(w_in.astype(BF16), ((0, 0), (0, pad)))
    proj = fused_matmul(res, w_in, gin=g0, tn=_pick_tn(width + pad, 512), name="attn_in")
    q, k, v, qi, ki, wi = rope_split(proj, dq, dkv, dqi)
    mask = index_mask(qi, ki, wi, min(TOPK_MAX, L // 4))
    o = masked_attention(q, k, v, mask)
    return fused_matmul(o, w_out.astype(BF16), res=res, gout=g1, name="attn_out")


def s5_layer(res, g0, g1, lam_re, lam_im, log_dt, b_re, b_im, c_re, c_im, d_skip, w_glu):
    y = s5_core(res, g0, lam_re, lam_im, log_dt, b_re, b_im, c_re, c_im, d_skip)
    return fused_matmul(y, w_glu.astype(BF16), glu=True, res=res, gout=g1, name="s5_glu")


def kernel(x, norm_gains, mlp_w_up, mlp_w_down, conv_w_in, conv_b_in, conv_w_dw, conv_b_dw, conv_ln_g, conv_ln_b, conv_w_out, pool_w, pool_scale, attn_w_in, attn_w_out, ssm_lambda_re, ssm_lambda_im, ssm_log_dt, ssm_b_re, ssm_b_im, ssm_c_re, ssm_c_im, ssm_d, ssm_w_glu):
    B, L, D = x.shape
    depth = norm_gains.shape[0]
    outs = []
    for b in range(B):
        res = x[b]
        for i in range(depth):
            m, j = i % 4, i // 4
            g = norm_gains[i]
            if m == 0:
                res = conv_layer(res, g[0], g[1], conv_w_in[j], conv_b_in[j], conv_w_dw[j], conv_b_dw[j],
                                 conv_ln_g[j], conv_ln_b[j], conv_w_out[j])
            elif m == 1:
                res = pool_layer(res, g[0], pool_w[j].astype(BF16), pool_scale[j], g[1])
            elif m == 2:
                res = attn_layer(res, g[0], g[1], attn_w_in[j], attn_w_out[j])
            else:
                res = s5_layer(res, g[0], g[1], ssm_lambda_re[j], ssm_lambda_im[j], ssm_log_dt[j], ssm_b_re[j],
                               ssm_b_im[j], ssm_c_re[j], ssm_c_im[j], ssm_d[j], ssm_w_glu[j])
            res = mlp_layer(res, g[2], mlp_w_up[i].astype(BF16), mlp_w_down[i].astype(BF16), g[3])
        outs.append(res)
    return jnp.stack(outs, axis=0)
```

```python
import functools
import math

import jax
import jax.numpy as jnp
from jax import lax
from jax.experimental import pallas as pl
from jax.experimental.pallas import tpu as pltpu

NORM_EPS = 1e-6
LN_EPS = 1e-5
CONV_WIDTH = 31
POOL_WINDOWS = (2, 4, 8, 16)
N_HEADS = 16
N_KV_HEADS = 4
Q_PER_KV = N_HEADS // N_KV_HEADS
IDX_HEADS = 16
IDX_DIM = 64
TOPK_MAX = 256
ROPE_THETA = 500000.0
SSM_GROUP = 16
SSM_STATE = 64

V7X_LANES = 128
V7X_SUBLANES = 8
V7X_VMEM_BYTES = 64 * 1024 * 1024
VMEM_LIMIT = V7X_VMEM_BYTES - 8 * 1024 * 1024

INT_MIN = -(2 ** 31)
NEG_BIG = -1e30
F32 = jnp.float32
BF16 = jnp.bfloat16


def _cparams(*sem):
    return pltpu.CompilerParams(dimension_semantics=sem, vmem_limit_bytes=VMEM_LIMIT)


def _rms(x, g):
    ms = jnp.mean(x * x, axis=-1, keepdims=True)
    return x * lax.rsqrt(ms + NORM_EPS) * g


def _pick_tn(n, cap):
    return max(t for t in range(V7X_LANES, cap + 1, V7X_LANES) if n % t == 0)


def _dot(a, b):
    return jnp.dot(a, b, preferred_element_type=F32)


def _fm_body(*refs, has_norm, has_bias, glu, has_epi, nj, tn):
    it = iter(refs)
    x_ref = next(it)
    gin_ref = next(it) if has_norm else None
    w1_ref = next(it)
    w2_ref = next(it) if glu else None
    b1_ref = next(it) if has_bias else None
    b2_ref = next(it) if (has_bias and glu) else None
    res_ref = next(it) if has_epi else None
    gout_ref = next(it) if has_epi else None
    out_ref = next(it)
    xn_ref = next(it)
    acc_ref = next(it) if has_epi else None

    j = pl.program_id(1)

    @pl.when(j == 0)
    def _():
        x = x_ref[...].astype(F32)
        if has_norm:
            x = _rms(x, gin_ref[...])
        xn_ref[...] = x.astype(BF16)

    xn = xn_ref[...]
    y = _dot(xn, w1_ref[...])
    if has_bias:
        y = y + b1_ref[...]
    if glu:
        y2 = _dot(xn, w2_ref[...])
        if has_bias:
            y2 = y2 + b2_ref[...]
        y = y * jax.nn.sigmoid(y2)
    if not has_epi:
        out_ref[...] = y.astype(out_ref.dtype)
    else:
        acc_ref[j] = y

        @pl.when(j == nj - 1)
        def _():
            ssq = jnp.zeros((acc_ref.shape[1], 1), F32)
            for jj in range(nj):
                a = acc_ref[jj]
                ssq = ssq + jnp.sum(a * a, axis=-1, keepdims=True)
            rs = lax.rsqrt(ssq / (nj * tn) + NORM_EPS)
            for jj in range(nj):
                sl = slice(jj * tn, (jj + 1) * tn)
                out_ref[:, sl] = res_ref[:, sl] + acc_ref[jj] * rs * gout_ref[:, sl]


def fused_matmul(x, w, *, glu=False, gin=None, b=None, res=None, gout=None,
                 out_dtype=F32, tm=512, tn=512, name="fused_matmul"):
    L, K = x.shape
    N = w.shape[1] // 2 if glu else w.shape[1]
    tn = min(tn, N)
    assert L % tm == 0 and N % tn == 0
    nj = N // tn
    has_norm, has_bias, has_epi = gin is not None, b is not None, res is not None

    args = [x]
    specs = [pl.BlockSpec((tm, K), lambda i, j: (i, 0))]
    if has_norm:
        args.append(gin.reshape(1, K).astype(F32))
        specs.append(pl.BlockSpec((1, K), lambda i, j: (0, 0)))
    args.append(w)
    specs.append(pl.BlockSpec((K, tn), lambda i, j: (0, j)))
    if glu:
        args.append(w)
        specs.append(pl.BlockSpec((K, tn), lambda i, j: (0, j + nj)))
    if has_bias:
        b = b.reshape(1, -1).astype(F32)
        args.append(b)
        specs.append(pl.BlockSpec((1, tn), lambda i, j: (0, j)))
        if glu:
            args.append(b)
            specs.append(pl.BlockSpec((1, tn), lambda i, j: (0, j + nj)))
    scratch = [pltpu.VMEM((tm, K), BF16)]
    if has_epi:
        args += [res, gout.reshape(1, N).astype(F32)]
        specs += [pl.BlockSpec((tm, N), lambda i, j: (i, 0)), pl.BlockSpec((1, N), lambda i, j: (0, 0))]
        out_spec = pl.BlockSpec((tm, N), lambda i, j: (i, 0))
        out_dtype = F32
        scratch.append(pltpu.VMEM((nj, tm, tn), F32))
    else:
        out_spec = pl.BlockSpec((tm, tn), lambda i, j: (i, j))

    body = functools.partial(_fm_body, has_norm=has_norm, has_bias=has_bias, glu=glu,
                             has_epi=has_epi, nj=nj, tn=tn)
    return pl.pallas_call(
        body,
        grid=(L // tm, nj),
        in_specs=specs,
        out_specs=out_spec,
        out_shape=jax.ShapeDtypeStruct((L, N), out_dtype),
        scratch_shapes=scratch,
        compiler_params=_cparams("parallel", "arbitrary"),
        name=name,
    )(*args)


def _mlp_body(res_ref, gin_ref, wu_ref, wd_ref, gout_ref, out_ref, xn_ref, acc_ref, *, nj):
    j = pl.program_id(1)

    @pl.when(j == 0)
    def _():
        xn_ref[...] = _rms(res_ref[...], gin_ref[...]).astype(BF16)
        acc_ref[...] = jnp.zeros_like(acc_ref)

    h = _dot(xn_ref[...], wu_ref[...])
    h = jnp.maximum(h, 0.0)
    acc_ref[...] += _dot((h * h).astype(BF16), wd_ref[...])

    @pl.when(j == nj - 1)
    def _():
        out_ref[...] = res_ref[...] + _rms(acc_ref[...], gout_ref[...])


def mlp_layer(res, gin, w_up, w_down, gout, layer, *, tm=512, tf=512):
    L, D = res.shape
    F = w_up.shape[2]
    nj = F // tf
    return pl.pallas_call(
        functools.partial(_mlp_body, nj=nj),
        grid=(L // tm, nj),
        in_specs=[
            pl.BlockSpec((tm, D), lambda i, j: (i, 0)),
            pl.BlockSpec((1, D), lambda i, j: (0, 0)),
            pl.BlockSpec((None, D, tf), lambda i, j: (layer, 0, j)),
            pl.BlockSpec((None, tf, D), lambda i, j: (layer, j, 0)),
            pl.BlockSpec((1, D), lambda i, j: (0, 0)),
        ],
        out_specs=pl.BlockSpec((tm, D), lambda i, j: (i, 0)),
        out_shape=jax.ShapeDtypeStruct((L, D), F32),
        scratch_shapes=[pltpu.VMEM((tm, D), BF16), pltpu.VMEM((tm, D), F32)],
        compiler_params=_cparams("parallel", "arbitrary"),
        name="mlp",
    )(res, gin.reshape(1, D), w_up, w_down, gout.reshape(1, D))


CONV_HALO = 32
CONV_ROWS = 32


def _dwconv_body(cur_ref, halo_ref, w_ref, b_ref, lg_ref, lb_ref, out_ref, sh_ref, cv_ref, *, tm):
    i = pl.program_id(0)
    D = cur_ref.shape[1]
    sub, lanes = V7X_SUBLANES, V7X_LANES
    ncol = D // lanes
    row_chunks = range(0, tm, CONV_ROWS)
    halo = halo_ref[...]
    halo = jnp.where(i == 0, jnp.zeros_like(halo), halo)
    for c in range(ncol):
        sh_ref[c, 0, 0:CONV_HALO, :] = halo[:, c * lanes:(c + 1) * lanes]
        sh_ref[c, 0, CONV_HALO:, :] = cur_ref[:, c * lanes:(c + 1) * lanes]
    n = tm + CONV_HALO - sub

    def column(c, carry):
        for j in range(1, sub):
            sh_ref[c, j, sub:, :] = sh_ref[c, 0, sub - j:sub - j + n, :]
        accs = [jnp.zeros((CONV_ROWS, lanes), F32) + b_ref[c] for _ in row_chunks]
        for k in range(CONV_WIDTH):
            a, j = divmod(CONV_WIDTH - 1 - k, sub)
            wk = jnp.broadcast_to(w_ref[c, k:k + 1, :], (CONV_ROWS, lanes))
            for ri, r0 in enumerate(row_chunks):
                start = CONV_HALO + r0 - sub * a
                accs[ri] = accs[ri] + sh_ref[c, j, start:start + CONV_ROWS, :] * wk
        for ri, r0 in enumerate(row_chunks):
            cv_ref[c, r0:r0 + CONV_ROWS, :] = accs[ri]
        return carry

    lax.fori_loop(0, ncol, column, 0)

    s1 = cv_ref[0]
    for c in range(1, ncol):
        s1 = s1 + cv_ref[c]
    mu = jnp.sum(s1, axis=-1, keepdims=True) / D
    s2 = jnp.zeros((tm, lanes), F32)
    for c in range(ncol):
        hc = cv_ref[c] - mu
        s2 = s2 + hc * hc
    rstd = lax.rsqrt(jnp.sum(s2, axis=-1, keepdims=True) / D + LN_EPS)
    for c in range(ncol):
        cs = slice(c * lanes, (c + 1) * lanes)
        y = (cv_ref[c] - mu) * rstd * lg_ref[:, cs] + lb_ref[:, cs]
        out_ref[:, cs] = (y * jax.nn.sigmoid(y)).astype(out_ref.dtype)


def dwconv_ln_swish(h, w_dw, b_dw, ln_g, ln_b, *, tm=256):
    L, D = h.shape
    hb = tm // CONV_HALO
    ncol = D // V7X_LANES
    w_cols = w_dw.reshape(CONV_WIDTH, ncol, V7X_LANES).transpose(1, 0, 2)
    b_cols = b_dw.reshape(ncol, 1, V7X_LANES)
    return pl.pallas_call(
        functools.partial(_dwconv_body, tm=tm),
        grid=(L // tm,),
        in_specs=[
            pl.BlockSpec((tm, D), lambda i: (i, 0)),
            pl.BlockSpec((CONV_HALO, D), lambda i: (jnp.maximum(i * hb - 1, 0), 0)),
            pl.BlockSpec((ncol, CONV_WIDTH, V7X_LANES), lambda i: (0, 0, 0)),
            pl.BlockSpec((ncol, 1, V7X_LANES), lambda i: (0, 0, 0)),
            pl.BlockSpec((1, D), lambda i: (0, 0)),
            pl.BlockSpec((1, D), lambda i: (0, 0)),
        ],
        out_specs=pl.BlockSpec((tm, D), lambda i: (i, 0)),
        out_shape=jax.ShapeDtypeStruct((L, D), BF16),
        scratch_shapes=[pltpu.VMEM((ncol, V7X_SUBLANES, tm + CONV_HALO, V7X_LANES), F32),
                        pltpu.VMEM((ncol, tm, V7X_LANES), F32)],
        compiler_params=_cparams("parallel"),
        name="dwconv_ln_swish",
    )(h, h, w_cols, b_cols, ln_g.reshape(1, D), ln_b.reshape(1, D))


POOL_HALO = 16


def _pool_body(cur_ref, halo_ref, gin_ref, pw_ref, ps_ref, gout_ref, out_ref, ext_ref, mix_ref, *, tm):
    i = pl.program_id(0)
    D = cur_ref.shape[1]
    cg = D // len(POOL_WINDOWS)
    hn = _rms(halo_ref[...], gin_ref[...])
    ext_ref[0:POOL_HALO, :] = jnp.where(i == 0, jnp.zeros_like(hn), hn)
    ext_ref[POOL_HALO:, :] = _rms(cur_ref[...], gin_ref[...])
    t1 = (i * tm + 1 + lax.broadcasted_iota(jnp.int32, (tm, 1), 0)).astype(F32)
    for g, w in enumerate(POOL_WINDOWS):
        cs = slice(g * cg, (g + 1) * cg)
        s = ext_ref[POOL_HALO:POOL_HALO + tm, cs]
        x = s
        for d in range(1, w):
            s = s + ext_ref[POOL_HALO - d:POOL_HALO - d + tm, cs]
        pooled = s / jnp.minimum(t1, float(w)) - x
        mix_ref[:, cs] = _dot(pooled.astype(BF16), pw_ref[g]) * ps_ref[:, cs]
    out_ref[...] = cur_ref[...] + _rms(mix_ref[...], gout_ref[...])


def pool_layer(res, gin, pool_w, pool_scale, gout, *, tm=256):
    L, D = res.shape
    ng = len(POOL_WINDOWS)
    cg = D // ng
    hb = tm // POOL_HALO
    return pl.pallas_call(
        functools.partial(_pool_body, tm=tm),
        grid=(L // tm,),
        in_specs=[
            pl.BlockSpec((tm, D), lambda i: (i, 0)),
            pl.BlockSpec((POOL_HALO, D), lambda i: (jnp.maximum(i * hb - 1, 0), 0)),
            pl.BlockSpec((1, D), lambda i: (0, 0)),
            pl.BlockSpec((ng, cg, cg), lambda i: (0, 0, 0)),
            pl.BlockSpec((1, D), lambda i: (0, 0)),
            pl.BlockSpec((1, D), lambda i: (0, 0)),
        ],
        out_specs=pl.BlockSpec((tm, D), lambda i: (i, 0)),
        out_shape=jax.ShapeDtypeStruct((L, D), F32),
        scratch_shapes=[pltpu.VMEM((tm + POOL_HALO, D), F32), pltpu.VMEM((tm, D), F32)],
        compiler_params=_cparams("parallel"),
        name="pool_layer",
    )(res, res, gin.reshape(1, D), pool_w, pool_scale.reshape(1, D), gout.reshape(1, D))


def _rope_tables(L, width, r):
    half = r // 2
    inv = jnp.power(ROPE_THETA, -2.0 * jnp.arange(half, dtype=F32) / r)
    ang = jnp.arange(L, dtype=F32)[:, None] * inv[None, :]
    cos, sin = jnp.cos(ang), jnp.sin(ang)
    pad = width - r
    cos_h = jnp.concatenate([cos, cos, jnp.ones((L, pad), F32)], axis=1)
    sin_h = jnp.concatenate([-sin, sin, jnp.zeros((L, pad), F32)], axis=1)
    reps = V7X_LANES // width
    return jnp.tile(cos_h, (1, reps)), jnp.tile(sin_h, (1, reps))


def _rope128(x, cos_t, sin_t, half, width):
    lane = lax.broadcasted_iota(jnp.int32, x.shape, 1) % width
    fwd = pltpu.roll(x, V7X_LANES - half, axis=1)
    bwd = pltpu.roll(x, half, axis=1)
    partner = jnp.where(lane < half, fwd, bwd)
    return x * cos_t + partner * sin_t


def _rope_body(p_ref, cq_ref, sq_ref, ci_ref, si_ref, q_ref, k_ref, v_ref, qi_ref, ki_ref, wi_ref,
               *, dq, dkv, dqi, q_scale, wi_scale):
    cq, sq, ci, si = cq_ref[...], sq_ref[...], ci_ref[...], si_ref[...]
    hd = V7X_LANES
    for h in range(dq // hd):
        x = p_ref[:, h * hd:(h + 1) * hd]
        q_ref[:, h * hd:(h + 1) * hd] = (_rope128(x, cq, sq, 16, hd) * q_scale).astype(BF16)
    o = dq
    for h in range(dkv // hd):
        x = p_ref[:, o + h * hd:o + (h + 1) * hd]
        k_ref[:, h * hd:(h + 1) * hd] = _rope128(x, cq, sq, 16, hd).astype(BF16)
    o += dkv
    v_ref[...] = p_ref[:, o:o + dkv].astype(BF16)
    o += dkv
    for h in range(dqi // hd):
        x = p_ref[:, o + h * hd:o + (h + 1) * hd]
        qi_ref[:, h * hd:(h + 1) * hd] = _rope128(x, ci, si, 8, IDX_DIM).astype(BF16)
    o += dqi
    x = p_ref[:, o:o + hd]
    xr = _rope128(x, ci, si, 8, IDX_DIM)
    ki_ref[...] = xr[:, :IDX_DIM].astype(BF16)
    wi_ref[...] = x[:, IDX_DIM:IDX_DIM + IDX_HEADS] * wi_scale


def rope_split(proj, dq, dkv, dqi, *, tm=256):
    L, W = proj.shape
    cq, sq = _rope_tables(L, V7X_LANES, 32)
    ci, si = _rope_tables(L, IDX_DIM, 16)
    tab = pl.BlockSpec((tm, V7X_LANES), lambda i: (i, 0))
    row = lambda n: pl.BlockSpec((tm, n), lambda i: (i, 0))
    body = functools.partial(_rope_body, dq=dq, dkv=dkv, dqi=dqi, q_scale=V7X_LANES ** -0.5 * LOG2E,
                             wi_scale=IDX_HEADS ** -0.5 * IDX_DIM ** -0.5)
    return pl.pallas_call(
        body,
        grid=(L // tm,),
        in_specs=[row(W), tab, tab, tab, tab],
        out_specs=[row(dq), row(dkv), row(dkv), row(dqi), row(IDX_DIM), row(IDX_HEADS)],
        out_shape=[
            jax.ShapeDtypeStruct((L, dq), BF16), jax.ShapeDtypeStruct((L, dkv), BF16),
            jax.ShapeDtypeStruct((L, dkv), BF16), jax.ShapeDtypeStruct((L, dqi), BF16),
            jax.ShapeDtypeStruct((L, IDX_DIM), BF16), jax.ShapeDtypeStruct((L, IDX_HEADS), F32),
        ],
        compiler_params=_cparams("parallel"),
        name="rope_split",
    )(proj, cq, sq, ci, si)


IDX_TQ = 128
IDX_CK = 512


def _sortable(x):
    b = pltpu.bitcast(x + 0.0, jnp.int32)
    return jnp.where(b < 0, b ^ jnp.int32(0x7FFFFFFF), b)


def _index_body(qi_ref, kit_ref, wi_ref, mask_ref, keys_ref, jmax_ref, qs_ref, *, topk, nck):
    i = pl.program_id(0)
    tq, ck = IDX_TQ, IDX_CK
    n_act = (i * tq + tq + ck - 1) // ck
    tpos = i * tq + lax.broadcasted_iota(jnp.int32, (tq, 1), 0)
    wi = wi_ref[...]
    for h in range(IDX_HEADS):
        qs_ref[h * tq:(h + 1) * tq, :] = qi_ref[:, h * IDX_DIM:(h + 1) * IDX_DIM]

    def fill(c, carry):
        sc = _dot(qs_ref[...], kit_ref[c])
        acc = jnp.zeros((tq, ck), F32)
        for h in range(IDX_HEADS):
            acc = acc + jnp.maximum(sc[h * tq:(h + 1) * tq], 0.0) * wi[:, h:h + 1]
        spos = c * ck + lax.broadcasted_iota(jnp.int32, (tq, ck), 1)
        keys_ref[c] = jnp.where(spos <= tpos, _sortable(acc), INT_MIN)
        return carry

    lax.fori_loop(0, n_act, fill, 0)

    def count(pred):
        def step(c, part):
            m = jnp.where(pred(keys_ref[c], c), 1, 0)
            for l0 in range(0, ck, V7X_LANES):
                part = part + m[:, l0:l0 + V7X_LANES]
            return part
        part = lax.fori_loop(0, n_act, step, jnp.zeros((tq, V7X_LANES), jnp.int32))
        return jnp.sum(part, axis=-1, keepdims=True)

    thr = jnp.where(count(lambda kc, c: kc >= 0) >= topk, jnp.int32(0), jnp.int32(INT_MIN))

    def bit_step(b, thr):
        cand = thr | (jnp.int32(1) << (30 - b))
        return jnp.where(count(lambda kc, c: kc >= cand) >= topk, cand, thr)

    thr = lax.fori_loop(0, 31, bit_step, thr)

    need = topk - count(lambda kc, c: kc > thr)
    n_eq = count(lambda kc, c: kc == thr)
    jmax_ref[...] = jnp.full(jmax_ref.shape, nck * ck, jnp.int32)

    @pl.when(jnp.max(n_eq - need) > 0)
    def _():
        def eq_upto(j):
            def pred(kc, c):
                spos = c * ck + lax.broadcasted_iota(jnp.int32, (tq, ck), 1)
                return (kc == thr) & (spos <= j)
            return count(pred)

        nbits = (nck * ck - 1).bit_length()

        def jstep(b, j):
            cand = j & ~(jnp.int32(1) << (nbits - 1 - b))
            return jnp.where(eq_upto(cand) >= need, cand, j)

        j0 = jnp.full((tq, 1), (1 << nbits) - 1, jnp.int32)
        j = lax.fori_loop(0, nbits, jstep, j0)
        jmax_ref[...] = jnp.broadcast_to(j, jmax_ref.shape)

    jmax = jmax_ref[:, 0:1]
    for c in range(nck):
        @pl.when(c < n_act)
        def _():
            kc = keys_ref[c]
            spos = c * ck + lax.broadcasted_iota(jnp.int32, (tq, ck), 1)
            sel = ((kc > thr) | ((kc == thr) & (spos <= jmax))) & (kc > INT_MIN)
            mask_ref[:, c * ck:(c + 1) * ck] = jnp.where(sel, 1, 0).astype(jnp.int8)

        @pl.when(c >= n_act)
        def _():
            mask_ref[:, c * ck:(c + 1) * ck] = jnp.zeros((tq, ck), jnp.int8)


def index_mask(qi, ki, wi, topk):
    L = qi.shape[0]
    nck = L // IDX_CK
    kit = ki.reshape(nck, IDX_CK, IDX_DIM).transpose(0, 2, 1)
    return pl.pallas_call(
        functools.partial(_index_body, topk=topk, nck=nck),
        grid=(L // IDX_TQ,),
        in_specs=[
            pl.BlockSpec((IDX_TQ, IDX_HEADS * IDX_DIM), lambda i: (i, 0)),
            pl.BlockSpec((nck, IDX_DIM, IDX_CK), lambda i: (0, 0, 0)),
            pl.BlockSpec((IDX_TQ, IDX_HEADS), lambda i: (i, 0)),
        ],
        out_specs=pl.BlockSpec((IDX_TQ, L), lambda i: (i, 0)),
        out_shape=jax.ShapeDtypeStruct((L, L), jnp.int8),
        scratch_shapes=[pltpu.VMEM((nck, IDX_TQ, IDX_CK), jnp.int32), pltpu.VMEM((IDX_TQ, V7X_LANES), jnp.int32),
                        pltpu.VMEM((IDX_HEADS * IDX_TQ, IDX_DIM), BF16)],
        compiler_params=_cparams("parallel"),
        name="index_mask",
    )(qi, kit, wi)


ATT_TQ = 256
ATT_TK = 512
ATT_RB = 32
ATT_PV_ROWS = 128
LOG2E = 1.4426950408889634


def _attn_body(qb_ref, kc_ref, q_ref, k_ref, v_ref, mask_ref, o_ref, m_ref, l_ref, acc_ref, s_ref, p_ref, a_ref,
               bias_ref):
    step = pl.program_id(0)
    i = qb_ref[step]
    c = kc_ref[step]
    tq, tk, hd, rb = ATT_TQ, ATT_TK, V7X_LANES, ATT_RB
    rows = Q_PER_KV * tq
    nlc = tk // hd

    @pl.when(c == 0)
    def _():
        m_ref[...] = jnp.full_like(m_ref, NEG_BIG)
        l_ref[...] = jnp.zeros_like(l_ref)
        acc_ref[...] = jnp.zeros_like(acc_ref)

    bias_ref[...] = jnp.where(mask_ref[...].astype(F32) > 0.5, 0.0, NEG_BIG)
    for n in range(N_KV_HEADS):
        sb, pb, ab = s_ref.at[n % 2], p_ref.at[n % 2], a_ref.at[n % 2]
        qs = jnp.concatenate(
            [q_ref[:, (n * Q_PER_KV + g) * hd:(n * Q_PER_KV + g + 1) * hd] for g in range(Q_PER_KV)], axis=0)
        kn = k_ref[:, n * hd:(n + 1) * hd]
        s = lax.dot_general(qs, kn, (((1,), (1,)), ((), ())), preferred_element_type=F32)
        sb[...] = (s.reshape(Q_PER_KV, tq, tk) + bias_ref[...][None]).reshape(rows, tk)

        for r0 in range(0, rows, rb):
            hr = slice(n * rows + r0, n * rows + r0 + rb)
            x = sb[r0:r0 + rb, :]
            mx = x[:, 0:hd]
            for j in range(1, nlc):
                mx = jnp.maximum(mx, x[:, j * hd:(j + 1) * hd])
            m_prev = m_ref[hr, :]
            m_new = jnp.maximum(m_prev, jnp.max(mx, axis=-1, keepdims=True))
            ab[r0:r0 + rb, :] = jnp.exp2(m_prev - m_new)
            m_ref[hr, :] = m_new

        def exp_rows(t0):
            for r0 in range(t0, t0 + ATT_PV_ROWS, rb):
                hr = slice(n * rows + r0, n * rows + r0 + rb)
                m_new = m_ref[hr, :]
                psum = jnp.zeros((rb, hd), F32)
                for j in range(nlc):
                    p = jnp.exp2(sb[r0:r0 + rb, j * hd:(j + 1) * hd] - m_new)
                    psum = psum + p
                    pb[r0:r0 + rb, j * hd:(j + 1) * hd] = p.astype(BF16)
                l_ref[hr, :] = ab[r0:r0 + rb, :] * l_ref[hr, :] + psum

        def pv_rows(t0):
            hr = slice(n * rows + t0, n * rows + t0 + ATT_PV_ROWS)
            tr = slice(t0, t0 + ATT_PV_ROWS)
            acc_ref[hr, :] = ab[tr, :] * acc_ref[hr, :] + _dot(pb[tr, :], v_ref[:, n * hd:(n + 1) * hd])

        exp_rows(0)
        for t0 in range(ATT_PV_ROWS, rows, ATT_PV_ROWS):
            exp_rows(t0)
            pv_rows(t0 - ATT_PV_ROWS)
        pv_rows(rows - ATT_PV_ROWS)

    @pl.when(c == (i * tq + tq - 1) // tk)
    def _():
        for h in range(N_HEADS):
            rs = slice(h * tq, (h + 1) * tq)
            l = jnp.sum(l_ref[rs, :], axis=-1, keepdims=True)
            o_ref[:, h * hd:(h + 1) * hd] = (acc_ref[rs, :] / l).astype(o_ref.dtype)


def masked_attention(q, k, v, mask):
    L, dq = q.shape
    dkv = k.shape[1]
    tq, tk = ATT_TQ, ATT_TK
    pairs = [(i, c) for i in range(L // tq) for c in range((i * tq + tq - 1) // tk + 1)]
    qb = jnp.asarray([p[0] for p in pairs], jnp.int32)
    kc = jnp.asarray([p[1] for p in pairs], jnp.int32)
    stat = pltpu.VMEM((N_HEADS * tq, V7X_LANES), F32)
    grid_spec = pltpu.PrefetchScalarGridSpec(
        num_scalar_prefetch=2,
        grid=(len(pairs),),
        in_specs=[
            pl.BlockSpec((tq, dq), lambda s, qb, kc: (qb[s], 0)),
            pl.BlockSpec((tk, dkv), lambda s, qb, kc: (kc[s], 0)),
            pl.BlockSpec((tk, dkv), lambda s, qb, kc: (kc[s], 0)),
            pl.BlockSpec((tq, tk), lambda s, qb, kc: (qb[s], kc[s])),
        ],
        out_specs=pl.BlockSpec((tq, dq), lambda s, qb, kc: (qb[s], 0)),
        scratch_shapes=[stat, stat, stat,
                        pltpu.VMEM((2, Q_PER_KV * tq, tk), F32), pltpu.VMEM((2, Q_PER_KV * tq, tk), BF16),
                        pltpu.VMEM((2, Q_PER_KV * tq, V7X_LANES), F32), pltpu.VMEM((tq, tk), F32)],
    )
    return pl.pallas_call(
        _attn_body,
        grid_spec=grid_spec,
        out_shape=jax.ShapeDtypeStruct((L, dq), BF16),
        compiler_params=_cparams("arbitrary"),
        name="masked_attention",
    )(qb, kc, q, k, v, mask)


S5_OCT = V7X_SUBLANES
S5_PAD = 8


def _s5_body(res_ref, gin_ref, b_ref, c_ref, ar_ref, ai_ref, d_ref, y_ref, sr_ref, si_ref, hr_ref, hi_ref,
             u_ref, *, tc, nm):
    t_blk = pl.program_id(0)
    pitch = tc + S5_PAD
    uo = u_ref.shape[1] // S5_OCT
    so = nm * V7X_LANES

    @pl.when(t_blk == 0)
    def _():
        hr_ref[...] = jnp.zeros_like(hr_ref)
        hi_ref[...] = jnp.zeros_like(hi_ref)

    u_ref[...] = _rms(res_ref[...], gin_ref[...])
    for s in range(S5_OCT):
        bu = _dot(u_ref[:, s * uo:(s + 1) * uo].astype(BF16), b_ref[s])
        for m in range(nm):
            sr_ref[m, s * pitch:s * pitch + tc, :] = bu[:, m * V7X_LANES:(m + 1) * V7X_LANES]
            si_ref[m, s * pitch:s * pitch + tc, :] = bu[:, so + m * V7X_LANES:so + (m + 1) * V7X_LANES]

    ar = [ar_ref[m] for m in range(nm)]
    ai = [ai_ref[m] for m in range(nm)]

    def step(t, carry):
        hr, hi = carry
        nr, ni = [], []
        for m in range(nm):
            idx = (m, pl.ds(t, S5_OCT, stride=pitch), slice(None))
            r = ar[m] * hr[m] - ai[m] * hi[m] + sr_ref[idx]
            im = ar[m] * hi[m] + ai[m] * hr[m] + si_ref[idx]
            sr_ref[idx] = r
            si_ref[idx] = im
            nr.append(r)
            ni.append(im)
        return tuple(nr), tuple(ni)

    init = (tuple(hr_ref[m] for m in range(nm)), tuple(hi_ref[m] for m in range(nm)))
    hr, hi = lax.fori_loop(0, tc, step, init)
    for m in range(nm):
        hr_ref[m] = hr[m]
        hi_ref[m] = hi[m]

    for s in range(S5_OCT):
        rows = slice(s * pitch, s * pitch + tc)
        h = jnp.concatenate([sr_ref[m, rows, :] for m in range(nm)] + [si_ref[m, rows, :] for m in range(nm)], axis=1)
        cs = slice(s * uo, (s + 1) * uo)
        y = _dot(h.astype(BF16), c_ref[s]) + d_ref[:, cs] * u_ref[:, cs]
        y_ref[:, cs] = jax.nn.gelu(y).astype(y_ref.dtype)


def s5_core(res, gin, lam_re, lam_im, log_dt, b_re, b_im, c_re, c_im, d_skip, *, tc=256):
    L, D = res.shape
    G, P = lam_re.shape
    go = G // S5_OCT
    so = go * P
    nm = so // V7X_LANES
    uo = go * SSM_GROUP
    dt = jnp.exp(log_dt)[:, None]
    mag = jnp.exp(lam_re * dt)
    ar = mag * jnp.cos(lam_im * dt)
    ai = mag * jnp.sin(lam_im * dt)
    den = lam_re * lam_re + lam_im * lam_im
    cr = ((ar - 1.0) * lam_re + ai * lam_im) / den
    ci = (ai * lam_re - (ar - 1.0) * lam_im) / den
    bbr = cr[..., None] * b_re - ci[..., None] * b_im
    bbi = cr[..., None] * b_im + ci[..., None] * b_re
    eye = jnp.eye(go, dtype=F32)

    def blockdiag_in(bb):
        bb = bb.reshape(S5_OCT, go, P, SSM_GROUP)
        return jnp.einsum('sgpc,gh->sgchp', bb, eye).reshape(S5_OCT, uo, so)

    def blockdiag_out(cc):
        cc = cc.reshape(S5_OCT, go, SSM_GROUP, P)
        return jnp.einsum('sgcp,gh->sgphc', cc, eye).reshape(S5_OCT, so, uo)

    bmat = jnp.concatenate([blockdiag_in(bbr), blockdiag_in(bbi)], axis=2).astype(BF16)
    cmat = jnp.concatenate([blockdiag_out(c_re), blockdiag_out(-c_im)], axis=1).astype(BF16)
    a_r = ar.reshape(S5_OCT, nm, V7X_LANES).transpose(1, 0, 2)
    a_i = ai.reshape(S5_OCT, nm, V7X_LANES).transpose(1, 0, 2)

    pitch = tc + S5_PAD
    const3 = lambda t: (0, 0, 0)
    return pl.pallas_call(
        functools.partial(_s5_body, tc=tc, nm=nm),
        grid=(L // tc,),
        in_specs=[
            pl.BlockSpec((tc, D), lambda t: (t, 0)),
            pl.BlockSpec((1, D), lambda t: (0, 0)),
            pl.BlockSpec((S5_OCT, uo, 2 * so), const3, pipeline_mode=pl.Buffered(1)),
            pl.BlockSpec((S5_OCT, 2 * so, uo), const3, pipeline_mode=pl.Buffered(1)),
            pl.BlockSpec((nm, S5_OCT, V7X_LANES), const3),
            pl.BlockSpec((nm, S5_OCT, V7X_LANES), const3),
            pl.BlockSpec((1, D), lambda t: (0, 0)),
        ],
        out_specs=pl.BlockSpec((tc, D), lambda t: (t, 0)),
        out_shape=jax.ShapeDtypeStruct((L, D), BF16),
        scratch_shapes=[
            pltpu.VMEM((nm, S5_OCT * pitch, V7X_LANES), F32),
            pltpu.VMEM((nm, S5_OCT * pitch, V7X_LANES), F32),
            pltpu.VMEM((nm, S5_OCT, V7X_LANES), F32),
            pltpu.VMEM((nm, S5_OCT, V7X_LANES), F32),
            pltpu.VMEM((tc, D), F32),
        ],
        compiler_params=_cparams("arbitrary"),
        name="s5_core",
    )(res, gin.reshape(1, D), bmat, cmat, a_r, a_i, d_skip.reshape(1, D))


def conv_layer(res, g0, g1, w_in, b_in, w_dw, b_dw, ln_g, ln_b, w_out):
    h = fused_matmul(res, w_in.astype(BF16), glu=True, gin=g0, b=b_in, name="conv_in_glu")
    h = dwconv_ln_swish(h, w_dw, b_dw, ln_g, ln_b)
    return fused_matmul(h, w_out.astype(BF16), res=res, gout=g1, name="conv_out")


def attn_layer(res, g0, g1, w_in, w_out):
    L, D = res.shape
    dq = D
    dkv = N_KV_HEADS * (D // N_HEADS)
    dqi = IDX_HEADS * IDX_DIM
    width = w_in.shape[1]
    pad = (-width) % V7X_LANES
    w_in = jnp.pad(w_in.astype(BF16), ((0, 0), (0, pad)))
    proj = fused_matmul(res, w_in, gin=g0, tn=_pick_tn(width + pad, 512), name="attn_in")
    q, k, v, qi, ki, wi = rope_split(proj, dq, dkv, dqi)
    mask = index_mask(qi, ki, wi, min(TOPK_MAX, L // 4))
    o = masked_attention(q, k, v, mask)
    return fused_matmul(o, w_out.astype(BF16), res=res, gout=g1, name="attn_out")


def s5_layer(res, g0, g1, lam_re, lam_im, log_dt, b_re, b_im, c_re, c_im, d_skip, w_glu):
    y = s5_core(res, g0, lam_re, lam_im, log_dt, b_re, b_im, c_re, c_im, d_skip)
    return fused_matmul(y, w_glu.astype(BF16), glu=True, res=res, gout=g1, name="s5_glu")


def kernel(x, norm_gains, mlp_w_up, mlp_w_down, conv_w_in, conv_b_in, conv_w_dw, conv_b_dw, conv_ln_g, conv_ln_b, conv_w_out, pool_w, pool_scale, attn_w_in, attn_w_out, ssm_lambda_re, ssm_lambda_im, ssm_log_dt, ssm_b_re, ssm_b_im, ssm_c_re, ssm_c_im, ssm_d, ssm_w_glu):
    B, L, D = x.shape
    depth = norm_gains.shape[0]
    w_up = mlp_w_up.astype(BF16)
    w_down = mlp_w_down.astype(BF16)
    outs = []
    for b in range(B):
        res = x.reshape(L, D) if B == 1 else x[b]
        for i in range(depth):
            m, j = i % 4, i // 4
            g = norm_gains[i]
            if m == 0:
                res = conv_layer(res, g[0], g[1], conv_w_in[j], conv_b_in[j], conv_w_dw[j], conv_b_dw[j],
                                 conv_ln_g[j], conv_ln_b[j], conv_w_out[j])
            elif m == 1:
                res = pool_layer(res, g[0], pool_w[j].astype(BF16), pool_scale[j], g[1])
            elif m == 2:
                res = attn_layer(res, g[0], g[1], attn_w_in[j], attn_w_out[j])
            else:
                res = s5_layer(res, g[0], g[1], ssm_lambda_re[j], ssm_lambda_im[j], ssm_log_dt[j], ssm_b_re[j],
                               ssm_b_im[j], ssm_c_re[j], ssm_c_im[j], ssm_d[j], ssm_w_glu[j])
            res = mlp_layer(res, g[2], w_up, w_down, g[3], i)
        outs.append(res)
    return outs[0].reshape(1, L, D) if B == 1 else jnp.stack(outs, axis=0)
```

```python
import functools
import math

import jax
import jax.numpy as jnp
from jax import lax
from jax.experimental import pallas as pl
from jax.experimental.pallas import tpu as pltpu

NORM_EPS = 1e-6
LN_EPS = 1e-5
CONV_WIDTH = 31
POOL_WINDOWS = (2, 4, 8, 16)
N_HEADS = 16
N_KV_HEADS = 4
Q_PER_KV = N_HEADS // N_KV_HEADS
IDX_HEADS = 16
IDX_DIM = 64
TOPK_MAX = 256
ROPE_THETA = 500000.0
SSM_GROUP = 16
SSM_STATE = 64

V7X_LANES = 128
V7X_SUBLANES = 8
V7X_VMEM_BYTES = 64 * 1024 * 1024
VMEM_LIMIT = V7X_VMEM_BYTES - 8 * 1024 * 1024

INT_MIN = -(2 ** 31)
NEG_BIG = -1e30
F32 = jnp.float32
BF16 = jnp.bfloat16


def _cparams(*sem):
    return pltpu.CompilerParams(dimension_semantics=sem, vmem_limit_bytes=VMEM_LIMIT)


def _rms(x, g):
    ms = jnp.mean(x * x, axis=-1, keepdims=True)
    return x * lax.rsqrt(ms + NORM_EPS) * g


def _pick_tn(n, cap):
    return max(t for t in range(V7X_LANES, cap + 1, V7X_LANES) if n % t == 0)


def _dot(a, b):
    return jnp.dot(a, b, preferred_element_type=F32)


def _fm_body(*refs, has_norm, has_bias, glu, has_epi, nj, tn):
    it = iter(refs)
    x_ref = next(it)
    gin_ref = next(it) if has_norm else None
    w1_ref = next(it)
    w2_ref = next(it) if glu else None
    b1_ref = next(it) if has_bias else None
    b2_ref = next(it) if (has_bias and glu) else None
    res_ref = next(it) if has_epi else None
    gout_ref = next(it) if has_epi else None
    out_ref = next(it)
    xn_ref = next(it)
    acc_ref = next(it) if has_epi else None

    j = pl.program_id(1)

    @pl.when(j == 0)
    def _():
        x = x_ref[...].astype(F32)
        if has_norm:
            x = _rms(x, gin_ref[...])
        xn_ref[...] = x.astype(BF16)

    xn = xn_ref[...]
    y = _dot(xn, w1_ref[...])
    if has_bias:
        y = y + b1_ref[...]
    if glu:
        y2 = _dot(xn, w2_ref[...])
        if has_bias:
            y2 = y2 + b2_ref[...]
        y = y * jax.nn.sigmoid(y2)
    if not has_epi:
        out_ref[...] = y.astype(out_ref.dtype)
    else:
        acc_ref[j] = y

        @pl.when(j == nj - 1)
        def _():
            ssq = jnp.zeros((acc_ref.shape[1], 1), F32)
            for jj in range(nj):
                a = acc_ref[jj]
                ssq = ssq + jnp.sum(a * a, axis=-1, keepdims=True)
            rs = lax.rsqrt(ssq / (nj * tn) + NORM_EPS)
            for jj in range(nj):
                sl = slice(jj * tn, (jj + 1) * tn)
                out_ref[:, sl] = res_ref[:, sl] + acc_ref[jj] * rs * gout_ref[:, sl]


def fused_matmul(x, w, *, glu=False, gin=None, b=None, res=None, gout=None,
                 out_dtype=F32, tm=512, tn=512, name="fused_matmul"):
    L, K = x.shape
    N = w.shape[1] // 2 if glu else w.shape[1]
    tn = min(tn, N)
    assert L % tm == 0 and N % tn == 0
    nj = N // tn
    has_norm, has_bias, has_epi = gin is not None, b is not None, res is not None

    args = [x]
    specs = [pl.BlockSpec((tm, K), lambda i, j: (i, 0))]
    if has_norm:
        args.append(gin.reshape(1, K).astype(F32))
        specs.append(pl.BlockSpec((1, K), lambda i, j: (0, 0)))
    args.append(w)
    specs.append(pl.BlockSpec((K, tn), lambda i, j: (0, j)))
    if glu:
        args.append(w)
        specs.append(pl.BlockSpec((K, tn), lambda i, j: (0, j + nj)))
    if has_bias:
        b = b.reshape(1, -1).astype(F32)
        args.append(b)
        specs.append(pl.BlockSpec((1, tn), lambda i, j: (0, j)))
        if glu:
            args.append(b)
            specs.append(pl.BlockSpec((1, tn), lambda i, j: (0, j + nj)))
    scratch = [pltpu.VMEM((tm, K), BF16)]
    if has_epi:
        args += [res, gout.reshape(1, N).astype(F32)]
        specs += [pl.BlockSpec((tm, N), lambda i, j: (i, 0)), pl.BlockSpec((1, N), lambda i, j: (0, 0))]
        out_spec = pl.BlockSpec((tm, N), lambda i, j: (i, 0))
        out_dtype = F32
        scratch.append(pltpu.VMEM((nj, tm, tn), F32))
    else:
        out_spec = pl.BlockSpec((tm, tn), lambda i, j: (i, j))

    body = functools.partial(_fm_body, has_norm=has_norm, has_bias=has_bias, glu=glu,
                             has_epi=has_epi, nj=nj, tn=tn)
    return pl.pallas_call(
        body,
        grid=(L // tm, nj),
        in_specs=specs,
        out_specs=out_spec,
        out_shape=jax.ShapeDtypeStruct((L, N), out_dtype),
        scratch_shapes=scratch,
        compiler_params=_cparams("parallel", "arbitrary"),
        name=name,
    )(*args)


def _mlp_body(res_ref, gin_ref, wu_ref, wd_ref, gout_ref, out_ref, xn_ref, acc_ref, *, nj):
    j = pl.program_id(1)

    @pl.when(j == 0)
    def _():
        xn_ref[...] = _rms(res_ref[...], gin_ref[...]).astype(BF16)
        acc_ref[...] = jnp.zeros_like(acc_ref)

    h = _dot(xn_ref[...], wu_ref[...])
    h = jnp.maximum(h, 0.0)
    acc_ref[...] += _dot((h * h).astype(BF16), wd_ref[...])

    @pl.when(j == nj - 1)
    def _():
        out_ref[...] = res_ref[...] + _rms(acc_ref[...], gout_ref[...])


def mlp_layer(res, gin, w_up, w_down, gout, layer, *, tm=512, tf=512):
    L, D = res.shape
    F = w_up.shape[2]
    nj = F // tf
    return pl.pallas_call(
        functools.partial(_mlp_body, nj=nj),
        grid=(L // tm, nj),
        in_specs=[
            pl.BlockSpec((tm, D), lambda i, j: (i, 0)),
            pl.BlockSpec((1, D), lambda i, j: (0, 0)),
            pl.BlockSpec((None, D, tf), lambda i, j: (layer, 0, j)),
            pl.BlockSpec((None, tf, D), lambda i, j: (layer, j, 0)),
            pl.BlockSpec((1, D), lambda i, j: (0, 0)),
        ],
        out_specs=pl.BlockSpec((tm, D), lambda i, j: (i, 0)),
        out_shape=jax.ShapeDtypeStruct((L, D), F32),
        scratch_shapes=[pltpu.VMEM((tm, D), BF16), pltpu.VMEM((tm, D), F32)],
        compiler_params=_cparams("parallel", "arbitrary"),
        name="mlp",
    )(res, gin.reshape(1, D), w_up, w_down, gout.reshape(1, D))


CONV_HALO = 32
CONV_ROWS = 32


def _dwconv_body(cur_ref, halo_ref, w_ref, b_ref, lg_ref, lb_ref, out_ref, sh_ref, cv_ref, *, tm):
    i = pl.program_id(0)
    D = cur_ref.shape[1]
    sub, lanes = V7X_SUBLANES, V7X_LANES
    ncol = D // lanes
    row_chunks = range(0, tm, CONV_ROWS)
    halo = halo_ref[...]
    halo = jnp.where(i == 0, jnp.zeros_like(halo), halo)
    for c in range(ncol):
        sh_ref[c, 0, 0:CONV_HALO, :] = halo[:, c * lanes:(c + 1) * lanes]
        sh_ref[c, 0, CONV_HALO:, :] = cur_ref[:, c * lanes:(c + 1) * lanes]
    n = tm + CONV_HALO - sub

    def column(c, carry):
        for j in range(1, sub):
            sh_ref[c, j, sub:, :] = sh_ref[c, 0, sub - j:sub - j + n, :]
        accs = [jnp.zeros((CONV_ROWS, lanes), F32) + b_ref[c] for _ in row_chunks]
        for k in range(CONV_WIDTH):
            a, j = divmod(CONV_WIDTH - 1 - k, sub)
            wk = jnp.broadcast_to(w_ref[c, k:k + 1, :], (CONV_ROWS, lanes))
            for ri, r0 in enumerate(row_chunks):
                start = CONV_HALO + r0 - sub * a
                accs[ri] = accs[ri] + sh_ref[c, j, start:start + CONV_ROWS, :] * wk
        for ri, r0 in enumerate(row_chunks):
            cv_ref[c, r0:r0 + CONV_ROWS, :] = accs[ri]
        return carry

    lax.fori_loop(0, ncol, column, 0)

    s1 = cv_ref[0]
    for c in range(1, ncol):
        s1 = s1 + cv_ref[c]
    mu = jnp.sum(s1, axis=-1, keepdims=True) / D
    s2 = jnp.zeros((tm, lanes), F32)
    for c in range(ncol):
        hc = cv_ref[c] - mu
        s2 = s2 + hc * hc
    rstd = lax.rsqrt(jnp.sum(s2, axis=-1, keepdims=True) / D + LN_EPS)
    for c in range(ncol):
        cs = slice(c * lanes, (c + 1) * lanes)
        y = (cv_ref[c] - mu) * rstd * lg_ref[:, cs] + lb_ref[:, cs]
        out_ref[:, cs] = (y * jax.nn.sigmoid(y)).astype(out_ref.dtype)


def dwconv_ln_swish(h, w_dw, b_dw, ln_g, ln_b, *, tm=256):
    L, D = h.shape
    hb = tm // CONV_HALO
    ncol = D // V7X_LANES
    w_cols = w_dw.reshape(CONV_WIDTH, ncol, V7X_LANES).transpose(1, 0, 2)
    b_cols = b_dw.reshape(ncol, 1, V7X_LANES)
    return pl.pallas_call(
        functools.partial(_dwconv_body, tm=tm),
        grid=(L // tm,),
        in_specs=[
            pl.BlockSpec((tm, D), lambda i: (i, 0)),
            pl.BlockSpec((CONV_HALO, D), lambda i: (jnp.maximum(i * hb - 1, 0), 0)),
            pl.BlockSpec((ncol, CONV_WIDTH, V7X_LANES), lambda i: (0, 0, 0)),
            pl.BlockSpec((ncol, 1, V7X_LANES), lambda i: (0, 0, 0)),
            pl.BlockSpec((1, D), lambda i: (0, 0)),
            pl.BlockSpec((1, D), lambda i: (0, 0)),
        ],
        out_specs=pl.BlockSpec((tm, D), lambda i: (i, 0)),
        out_shape=jax.ShapeDtypeStruct((L, D), BF16),
        scratch_shapes=[pltpu.VMEM((ncol, V7X_SUBLANES, tm + CONV_HALO, V7X_LANES), F32),
                        pltpu.VMEM((ncol, tm, V7X_LANES), F32)],
        compiler_params=_cparams("parallel"),
        name="dwconv_ln_swish",
    )(h, h, w_cols, b_cols, ln_g.reshape(1, D), ln_b.reshape(1, D))


POOL_HALO = 16


def _pool_body(cur_ref, halo_ref, gin_ref, pw_ref, ps_ref, gout_ref, out_ref, ext_ref, mix_ref, *, tm):
    i = pl.program_id(0)
    D = cur_ref.shape[1]
    cg = D // len(POOL_WINDOWS)
    hn = _rms(halo_ref[...], gin_ref[...])
    ext_ref[0:POOL_HALO, :] = jnp.where(i == 0, jnp.zeros_like(hn), hn)
    ext_ref[POOL_HALO:, :] = _rms(cur_ref[...], gin_ref[...])
    t1 = (i * tm + 1 + lax.broadcasted_iota(jnp.int32, (tm, 1), 0)).astype(F32)
    for g, w in enumerate(POOL_WINDOWS):
        cs = slice(g * cg, (g + 1) * cg)
        s = ext_ref[POOL_HALO:POOL_HALO + tm, cs]
        x = s
        for d in range(1, w):
            s = s + ext_ref[POOL_HALO - d:POOL_HALO - d + tm, cs]
        pooled = s / jnp.minimum(t1, float(w)) - x
        mix_ref[:, cs] = _dot(pooled.astype(BF16), pw_ref[g]) * ps_ref[:, cs]
    out_ref[...] = cur_ref[...] + _rms(mix_ref[...], gout_ref[...])


def pool_layer(res, gin, pool_w, pool_scale, gout, *, tm=256):
    L, D = res.shape
    ng = len(POOL_WINDOWS)
    cg = D // ng
    hb = tm // POOL_HALO
    return pl.pallas_call(
        functools.partial(_pool_body, tm=tm),
        grid=(L // tm,),
        in_specs=[
            pl.BlockSpec((tm, D), lambda i: (i, 0)),
            pl.BlockSpec((POOL_HALO, D), lambda i: (jnp.maximum(i * hb - 1, 0), 0)),
            pl.BlockSpec((1, D), lambda i: (0, 0)),
            pl.BlockSpec((ng, cg, cg), lambda i: (0, 0, 0)),
            pl.BlockSpec((1, D), lambda i: (0, 0)),
            pl.BlockSpec((1, D), lambda i: (0, 0)),
        ],
        out_specs=pl.BlockSpec((tm, D), lambda i: (i, 0)),
        out_shape=jax.ShapeDtypeStruct((L, D), F32),
        scratch_shapes=[pltpu.VMEM((tm + POOL_HALO, D), F32), pltpu.VMEM((tm, D), F32)],
        compiler_params=_cparams("parallel"),
        name="pool_layer",
    )(res, res, gin.reshape(1, D), pool_w, pool_scale.reshape(1, D), gout.reshape(1, D))


def _rope_tables(L, width, r):
    half = r // 2
    inv = jnp.power(ROPE_THETA, -2.0 * jnp.arange(half, dtype=F32) / r)
    ang = jnp.arange(L, dtype=F32)[:, None] * inv[None, :]
    cos, sin = jnp.cos(ang), jnp.sin(ang)
    pad = width - r
    cos_h = jnp.concatenate([cos, cos, jnp.ones((L, pad), F32)], axis=1)
    sin_h = jnp.concatenate([-sin, sin, jnp.zeros((L, pad), F32)], axis=1)
    reps = V7X_LANES // width
    return jnp.tile(cos_h, (1, reps)), jnp.tile(sin_h, (1, reps))


def _rope128(x, cos_t, sin_t, half, width):
    lane = lax.broadcasted_iota(jnp.int32, x.shape, 1) % width
    fwd = pltpu.roll(x, V7X_LANES - half, axis=1)
    bwd = pltpu.roll(x, half, axis=1)
    partner = jnp.where(lane < half, fwd, bwd)
    return x * cos_t + partner * sin_t


def _rope_body(p_ref, cq_ref, sq_ref, ci_ref, si_ref, q_ref, k_ref, v_ref, qi_ref, ki_ref, wi_ref,
               *, dq, dkv, dqi, q_scale, wi_scale):
    cq, sq, ci, si = cq_ref[...], sq_ref[...], ci_ref[...], si_ref[...]
    hd = V7X_LANES
    for h in range(dq // hd):
        x = p_ref[:, h * hd:(h + 1) * hd]
        q_ref[:, h * hd:(h + 1) * hd] = (_rope128(x, cq, sq, 16, hd) * q_scale).astype(BF16)
    o = dq
    for h in range(dkv // hd):
        x = p_ref[:, o + h * hd:o + (h + 1) * hd]
        k_ref[:, h * hd:(h + 1) * hd] = _rope128(x, cq, sq, 16, hd).astype(BF16)
    o += dkv
    v_ref[...] = p_ref[:, o:o + dkv].astype(BF16)
    o += dkv
    for h in range(dqi // hd):
        x = p_ref[:, o + h * hd:o + (h + 1) * hd]
        qi_ref[:, h * hd:(h + 1) * hd] = _rope128(x, ci, si, 8, IDX_DIM).astype(BF16)
    o += dqi
    x = p_ref[:, o:o + hd]
    xr = _rope128(x, ci, si, 8, IDX_DIM)
    ki_ref[...] = xr[:, :IDX_DIM].astype(BF16)
    wi_ref[...] = x[:, IDX_DIM:IDX_DIM + IDX_HEADS] * wi_scale


def rope_split(proj, dq, dkv, dqi, *, tm=256):
    L, W = proj.shape
    cq, sq = _rope_tables(L, V7X_LANES, 32)
    ci, si = _rope_tables(L, IDX_DIM, 16)
    tab = pl.BlockSpec((tm, V7X_LANES), lambda i: (i, 0))
    row = lambda n: pl.BlockSpec((tm, n), lambda i: (i, 0))
    body = functools.partial(_rope_body, dq=dq, dkv=dkv, dqi=dqi, q_scale=V7X_LANES ** -0.5 * LOG2E,
                             wi_scale=IDX_HEADS ** -0.5 * IDX_DIM ** -0.5)
    return pl.pallas_call(
        body,
        grid=(L // tm,),
        in_specs=[row(W), tab, tab, tab, tab],
        out_specs=[row(dq), row(dkv), row(dkv), row(dqi), row(IDX_DIM), row(IDX_HEADS)],
        out_shape=[
            jax.ShapeDtypeStruct((L, dq), BF16), jax.ShapeDtypeStruct((L, dkv), BF16),
            jax.ShapeDtypeStruct((L, dkv), BF16), jax.ShapeDtypeStruct((L, dqi), BF16),
            jax.ShapeDtypeStruct((L, IDX_DIM), BF16), jax.ShapeDtypeStruct((L, IDX_HEADS), F32),
        ],
        compiler_params=_cparams("parallel"),
        name="rope_split",
    )(proj, cq, sq, ci, si)


IDX_TQ = 128
IDX_CK = 512


def _sortable(x):
    b = pltpu.bitcast(x + 0.0, jnp.int32)
    return jnp.where(b < 0, b ^ jnp.int32(0x7FFFFFFF), b)


def _index_body(qi_ref, kit_ref, wi_ref, mask_ref, keys_ref, jmax_ref, qs_ref, *, topk, nck):
    i = pl.program_id(0)
    tq, ck = IDX_TQ, IDX_CK
    n_act = (i * tq + tq + ck - 1) // ck
    tpos = i * tq + lax.broadcasted_iota(jnp.int32, (tq, 1), 0)
    wi = wi_ref[...]
    for h in range(IDX_HEADS):
        qs_ref[h * tq:(h + 1) * tq, :] = qi_ref[:, h * IDX_DIM:(h + 1) * IDX_DIM]

    def fill(c, carry):
        sc = _dot(qs_ref[...], kit_ref[c])
        acc = jnp.zeros((tq, ck), F32)
        for h in range(IDX_HEADS):
            acc = acc + jnp.maximum(sc[h * tq:(h + 1) * tq], 0.0) * wi[:, h:h + 1]
        spos = c * ck + lax.broadcasted_iota(jnp.int32, (tq, ck), 1)
        keys_ref[c] = jnp.where(spos <= tpos, _sortable(acc), INT_MIN)
        return carry

    lax.fori_loop(0, n_act, fill, 0)

    lanes = V7X_LANES
    lane_id = lax.broadcasted_iota(jnp.int32, (tq, lanes), 1)

    def count(pred):
        def step(c, part):
            for l0 in range(0, ck, lanes):
                part = part + jnp.where(pred(keys_ref[c, :, l0:l0 + lanes], c * ck + l0), 1, 0)
            return part
        part = lax.fori_loop(0, n_act, step, jnp.zeros((tq, lanes), jnp.int32))
        return jnp.sum(part, axis=-1, keepdims=True)

    cnt0 = count(lambda kg, p0: kg >= 0)
    thr = jnp.broadcast_to(jnp.where(cnt0 >= topk, jnp.int32(0), jnp.int32(INT_MIN)), (tq, lanes))

    def bit_step(b, thr):
        cand = thr | (jnp.int32(1) << (30 - b))
        return jnp.where(count(lambda kg, p0: kg >= cand) >= topk, cand, thr)

    thr = lax.fori_loop(0, 31, bit_step, thr)

    need = topk - count(lambda kg, p0: kg > thr)
    n_eq = count(lambda kg, p0: kg == thr)
    jmax_ref[...] = jnp.full(jmax_ref.shape, nck * ck, jnp.int32)

    @pl.when(jnp.max(n_eq - need) > 0)
    def _():
        nbits = (nck * ck - 1).bit_length()

        def jstep(b, j):
            cand = j & ~(jnp.int32(1) << (nbits - 1 - b))
            n_upto = count(lambda kg, p0: (kg == thr) & (p0 + lane_id <= cand))
            return jnp.where(n_upto >= need, cand, j)

        j0 = jnp.full((tq, lanes), (1 << nbits) - 1, jnp.int32)
        jmax_ref[...] = lax.fori_loop(0, nbits, jstep, j0)

    jmax = jmax_ref[...]
    for c in range(nck):
        @pl.when(c < n_act)
        def _():
            for l0 in range(0, ck, lanes):
                kg = keys_ref[c, :, l0:l0 + lanes]
                sel = ((kg > thr) | ((kg == thr) & (c * ck + l0 + lane_id <= jmax))) & (kg > INT_MIN)
                mask_ref[:, c * ck + l0:c * ck + l0 + lanes] = jnp.where(sel, 1, 0).astype(jnp.int8)

        @pl.when(c >= n_act)
        def _():
            mask_ref[:, c * ck:(c + 1) * ck] = jnp.zeros((tq, ck), jnp.int8)


def index_mask(qi, ki, wi, topk):
    L = qi.shape[0]
    nck = L // IDX_CK
    kit = ki.reshape(nck, IDX_CK, IDX_DIM).transpose(0, 2, 1)
    return pl.pallas_call(
        functools.partial(_index_body, topk=topk, nck=nck),
        grid=(L // IDX_TQ,),
        in_specs=[
            pl.BlockSpec((IDX_TQ, IDX_HEADS * IDX_DIM), lambda i: (i, 0)),
            pl.BlockSpec((nck, IDX_DIM, IDX_CK), lambda i: (0, 0, 0)),
            pl.BlockSpec((IDX_TQ, IDX_HEADS), lambda i: (i, 0)),
        ],
        out_specs=pl.BlockSpec((IDX_TQ, L), lambda i: (i, 0)),
        out_shape=jax.ShapeDtypeStruct((L, L), jnp.int8),
        scratch_shapes=[pltpu.VMEM((nck, IDX_TQ, IDX_CK), jnp.int32), pltpu.VMEM((IDX_TQ, V7X_LANES), jnp.int32),
                        pltpu.VMEM((IDX_HEADS * IDX_TQ, IDX_DIM), BF16)],
        compiler_params=_cparams("parallel"),
        name="index_mask",
    )(qi, kit, wi)


ATT_TQ = 512
ATT_TK = 512
ATT_RB = 32
ATT_PV_ROWS = 128
LOG2E = 1.4426950408889634


def _attn_body(qb_ref, kc_ref, q_ref, k_ref, v_ref, mask_ref, o_ref, m_ref, l_ref, acc_ref, s_ref, p_ref, a_ref,
               bias_ref):
    step = pl.program_id(0)
    i = qb_ref[step]
    c = kc_ref[step]
    tq, tk, hd, rb = ATT_TQ, ATT_TK, V7X_LANES, ATT_RB
    rows = Q_PER_KV * tq
    nlc = tk // hd

    @pl.when(c == 0)
    def _():
        m_ref[...] = jnp.full_like(m_ref, NEG_BIG)
        l_ref[...] = jnp.zeros_like(l_ref)
        acc_ref[...] = jnp.zeros_like(acc_ref)

    bias_ref[...] = jnp.where(mask_ref[...].astype(F32) > 0.5, 0.0, NEG_BIG)
    for n in range(N_KV_HEADS):
        sb, pb, ab = s_ref.at[n % 2], p_ref.at[n % 2], a_ref.at[n % 2]
        qs = jnp.concatenate(
            [q_ref[:, (n * Q_PER_KV + g) * hd:(n * Q_PER_KV + g + 1) * hd] for g in range(Q_PER_KV)], axis=0)
        kn = k_ref[:, n * hd:(n + 1) * hd]
        s = lax.dot_general(qs, kn, (((1,), (1,)), ((), ())), preferred_element_type=F32)
        sb[...] = (s.reshape(Q_PER_KV, tq, tk) + bias_ref[...][None]).reshape(rows, tk)

        for r0 in range(0, rows, rb):
            hr = slice(n * rows + r0, n * rows + r0 + rb)
            x = sb[r0:r0 + rb, :]
            mx = x[:, 0:hd]
            for j in range(1, nlc):
                mx = jnp.maximum(mx, x[:, j * hd:(j + 1) * hd])
            m_prev = m_ref[hr, :]
            m_new = jnp.maximum(m_prev, jnp.max(mx, axis=-1, keepdims=True))
            ab[r0:r0 + rb, :] = jnp.exp2(m_prev - m_new)
            m_ref[hr, :] = m_new

        def exp_rows(t0):
            for r0 in range(t0, t0 + ATT_PV_ROWS, rb):
                hr = slice(n * rows + r0, n * rows + r0 + rb)
                m_new = m_ref[hr, :]
                psum = jnp.zeros((rb, hd), F32)
                for j in range(nlc):
                    p = jnp.exp2(sb[r0:r0 + rb, j * hd:(j + 1) * hd] - m_new)
                    psum = psum + p
                    pb[r0:r0 + rb, j * hd:(j + 1) * hd] = p.astype(BF16)
                l_ref[hr, :] = ab[r0:r0 + rb, :] * l_ref[hr, :] + psum

        def pv_rows(t0):
            hr = slice(n * rows + t0, n * rows + t0 + ATT_PV_ROWS)
            tr = slice(t0, t0 + ATT_PV_ROWS)
            acc_ref[hr, :] = ab[tr, :] * acc_ref[hr, :] + _dot(pb[tr, :], v_ref[:, n * hd:(n + 1) * hd])

        exp_rows(0)
        for t0 in range(ATT_PV_ROWS, rows, ATT_PV_ROWS):
            exp_rows(t0)
            pv_rows(t0 - ATT_PV_ROWS)
        pv_rows(rows - ATT_PV_ROWS)

    @pl.when(c == (i * tq + tq - 1) // tk)
    def _():
        for h in range(N_HEADS):
            rs = slice(h * tq, (h + 1) * tq)
            l = jnp.sum(l_ref[rs, :], axis=-1, keepdims=True)
            o_ref[:, h * hd:(h + 1) * hd] = (acc_ref[rs, :] / l).astype(o_ref.dtype)


def masked_attention(q, k, v, mask):
    L, dq = q.shape
    dkv = k.shape[1]
    tq, tk = ATT_TQ, ATT_TK
    pairs = [(i, c) for i in range(L // tq) for c in range((i * tq + tq - 1) // tk + 1)]
    qb = jnp.asarray([p[0] for p in pairs], jnp.int32)
    kc = jnp.asarray([p[1] for p in pairs], jnp.int32)
    stat = pltpu.VMEM((N_HEADS * tq, V7X_LANES), F32)
    grid_spec = pltpu.PrefetchScalarGridSpec(
        num_scalar_prefetch=2,
        grid=(len(pairs),),
        in_specs=[
            pl.BlockSpec((tq, dq), lambda s, qb, kc: (qb[s], 0)),
            pl.BlockSpec((tk, dkv), lambda s, qb, kc: (kc[s], 0)),
            pl.BlockSpec((tk, dkv), lambda s, qb, kc: (kc[s], 0)),
            pl.BlockSpec((tq, tk), lambda s, qb, kc: (qb[s], kc[s])),
        ],
        out_specs=pl.BlockSpec((tq, dq), lambda s, qb, kc: (qb[s], 0)),
        scratch_shapes=[stat, stat, stat,
                        pltpu.VMEM((2, Q_PER_KV * tq, tk), F32), pltpu.VMEM((2, Q_PER_KV * tq, tk), BF16),
                        pltpu.VMEM((2, Q_PER_KV * tq, V7X_LANES), F32), pltpu.VMEM((tq, tk), F32)],
    )
    return pl.pallas_call(
        _attn_body,
        grid_spec=grid_spec,
        out_shape=jax.ShapeDtypeStruct((L, dq), BF16),
        compiler_params=_cparams("arbitrary"),
        name="masked_attention",
    )(qb, kc, q, k, v, mask)


S5_OCT = V7X_SUBLANES
S5_PAD = 8


def _s5_body(res_ref, gin_ref, b_ref, c_ref, ar_ref, ai_ref, d_ref, y_ref, sr_ref, si_ref, hr_ref, hi_ref,
             u_ref, *, tc, nm):
    t_blk = pl.program_id(0)
    pitch = tc + S5_PAD
    uo = u_ref.shape[1] // S5_OCT
    so = nm * V7X_LANES

    @pl.when(t_blk == 0)
    def _():
        hr_ref[...] = jnp.zeros_like(hr_ref)
        hi_ref[...] = jnp.zeros_like(hi_ref)

    u_ref[...] = _rms(res_ref[...], gin_ref[...])
    for s in range(S5_OCT):
        bu = _dot(u_ref[:, s * uo:(s + 1) * uo].astype(BF16), b_ref[s])
        for m in range(nm):
            sr_ref[m, s * pitch:s * pitch + tc, :] = bu[:, m * V7X_LANES:(m + 1) * V7X_LANES]
            si_ref[m, s * pitch:s * pitch + tc, :] = bu[:, so + m * V7X_LANES:so + (m + 1) * V7X_LANES]

    ar = [ar_ref[m] for m in range(nm)]
    ai = [ai_ref[m] for m in range(nm)]

    def step(t, carry):
        hr, hi = carry
        nr, ni = [], []
        for m in range(nm):
            idx = (m, pl.ds(t, S5_OCT, stride=pitch), slice(None))
            r = ar[m] * hr[m] - ai[m] * hi[m] + sr_ref[idx]
            im = ar[m] * hi[m] + ai[m] * hr[m] + si_ref[idx]
            sr_ref[idx] = r
            si_ref[idx] = im
            nr.append(r)
            ni.append(im)
        return tuple(nr), tuple(ni)

    init = (tuple(hr_ref[m] for m in range(nm)), tuple(hi_ref[m] for m in range(nm)))
    hr, hi = lax.fori_loop(0, tc, step, init)
    for m in range(nm):
        hr_ref[m] = hr[m]
        hi_ref[m] = hi[m]

    for s in range(S5_OCT):
        rows = slice(s * pitch, s * pitch + tc)
        h = jnp.concatenate([sr_ref[m, rows, :] for m in range(nm)] + [si_ref[m, rows, :] for m in range(nm)], axis=1)
        cs = slice(s * uo, (s + 1) * uo)
        y = _dot(h.astype(BF16), c_ref[s]) + d_ref[:, cs] * u_ref[:, cs]
        y_ref[:, cs] = jax.nn.gelu(y).astype(y_ref.dtype)


def s5_core(res, gin, lam_re, lam_im, log_dt, b_re, b_im, c_re, c_im, d_skip, *, tc=256):
    L, D = res.shape
    G, P = lam_re.shape
    go = G // S5_OCT
    so = go * P
    nm = so // V7X_LANES
    uo = go * SSM_GROUP
    dt = jnp.exp(log_dt)[:, None]
    mag = jnp.exp(lam_re * dt)
    ar = mag * jnp.cos(lam_im * dt)
    ai = mag * jnp.sin(lam_im * dt)
    den = lam_re * lam_re + lam_im * lam_im
    cr = ((ar - 1.0) * lam_re + ai * lam_im) / den
    ci = (ai * lam_re - (ar - 1.0) * lam_im) / den
    bbr = cr[..., None] * b_re - ci[..., None] * b_im
    bbi = cr[..., None] * b_im + ci[..., None] * b_re
    eye = jnp.eye(go, dtype=F32)

    def blockdiag_in(bb):
        bb = bb.reshape(S5_OCT, go, P, SSM_GROUP)
        return jnp.einsum('sgpc,gh->sgchp', bb, eye).reshape(S5_OCT, uo, so)

    def blockdiag_out(cc):
        cc = cc.reshape(S5_OCT, go, SSM_GROUP, P)
        return jnp.einsum('sgcp,gh->sgphc', cc, eye).reshape(S5_OCT, so, uo)

    bmat = jnp.concatenate([blockdiag_in(bbr), blockdiag_in(bbi)], axis=2).astype(BF16)
    cmat = jnp.concatenate([blockdiag_out(c_re), blockdiag_out(-c_im)], axis=1).astype(BF16)
    a_r = ar.reshape(S5_OCT, nm, V7X_LANES).transpose(1, 0, 2)
    a_i = ai.reshape(S5_OCT, nm, V7X_LANES).transpose(1, 0, 2)

    pitch = tc + S5_PAD
    const3 = lambda t: (0, 0, 0)
    return pl.pallas_call(
        functools.partial(_s5_body, tc=tc, nm=nm),
        grid=(L // tc,),
        in_specs=[
            pl.BlockSpec((tc, D), lambda t: (t, 0)),
            pl.BlockSpec((1, D), lambda t: (0, 0)),
            pl.BlockSpec((S5_OCT, uo, 2 * so), const3, pipeline_mode=pl.Buffered(1)),
            pl.BlockSpec((S5_OCT, 2 * so, uo), const3, pipeline_mode=pl.Buffered(1)),
            pl.BlockSpec((nm, S5_OCT, V7X_LANES), const3),
            pl.BlockSpec((nm, S5_OCT, V7X_LANES), const3),
            pl.BlockSpec((1, D), lambda t: (0, 0)),
        ],
        out_specs=pl.BlockSpec((tc, D), lambda t: (t, 0)),
        out_shape=jax.ShapeDtypeStruct((L, D), BF16),
        scratch_shapes=[
            pltpu.VMEM((nm, S5_OCT * pitch, V7X_LANES), F32),
            pltpu.VMEM((nm, S5_OCT * pitch, V7X_LANES), F32),
            pltpu.VMEM((nm, S5_OCT, V7X_LANES), F32),
            pltpu.VMEM((nm, S5_OCT, V7X_LANES), F32),
            pltpu.VMEM((tc, D), F32),
        ],
        compiler_params=_cparams("arbitrary"),
        name="s5_core",
    )(res, gin.reshape(1, D), bmat, cmat, a_r, a_i, d_skip.reshape(1, D))


def conv_layer(res, g0, g1, w_in, b_in, w_dw, b_dw, ln_g, ln_b, w_out):
    h = fused_matmul(res, w_in.astype(BF16), glu=True, gin=g0, b=b_in, name="conv_in_glu")
    h = dwconv_ln_swish(h, w_dw, b_dw, ln_g, ln_b)
    return fused_matmul(h, w_out.astype(BF16), res=res, gout=g1, name="conv_out")


def attn_layer(res, g0, g1, w_in, w_out):
    L, D = res.shape
    dq = D
    dkv = N_KV_HEADS * (D // N_HEADS)
    dqi = IDX_HEADS * IDX_DIM
    width = w_in.shape[1]
    pad = (-width) % V7X_LANES
    w_in = jnp.pad(w_in.astype(BF16), ((0, 0), (0, pad)))
    proj = fused_matmul(res, w_in, gin=g0, tn=_pick_tn(width + pad, 512), name="attn_in")
    q, k, v, qi, ki, wi = rope_split(proj, dq, dkv, dqi)
    mask = index_mask(qi, ki, wi, min(TOPK_MAX, L // 4))
    o = masked_attention(q, k, v, mask)
    return fused_matmul(o, w_out.astype(BF16), res=res, gout=g1, name="attn_out")


def s5_layer(res, g0, g1, lam_re, lam_im, log_dt, b_re, b_im, c_re, c_im, d_skip, w_glu):
    y = s5_core(res, g0, lam_re, lam_im, log_dt, b_re, b_im, c_re, c_im, d_skip)
    return fused_matmul(y, w_glu.astype(BF16), glu=True, res=res, gout=g1, name="s5_glu")


def kernel(x, norm_gains, mlp_w_up, mlp_w_down, conv_w_in, conv_b_in, conv_w_dw, conv_b_dw, conv_ln_g, conv_ln_b, conv_w_out, pool_w, pool_scale, attn_w_in, attn_w_out, ssm_lambda_re, ssm_lambda_im, ssm_log_dt, ssm_b_re, ssm_b_im, ssm_c_re, ssm_c_im, ssm_d, ssm_w_glu):
    B, L, D = x.shape
    depth = norm_gains.shape[0]
    w_up = mlp_w_up.astype(BF16)
    w_down = mlp_w_down.astype(BF16)
    outs = []
    for b in range(B):
        res = x.reshape(L, D) if B == 1 else x[b]
        for i in range(depth):
            m, j = i % 4, i // 4
            g = norm_gains[i]
            if m == 0:
                res = conv_layer(res, g[0], g[1], conv_w_in[j], conv_b_in[j], conv_w_dw[j], conv_b_dw[j],
                                 conv_ln_g[j], conv_ln_b[j], conv_w_out[j])
            elif m == 1:
                res = pool_layer(res, g[0], pool_w[j].astype(BF16), pool_scale[j], g[1])
            elif m == 2:
                res = attn_layer(res, g[0], g[1], attn_w_in[j], attn_w_out[j])
            else:
                res = s5_layer(res, g[0], g[1], ssm_lambda_re[j], ssm_lambda_im[j], ssm_log_dt[j], ssm_b_re[j],
                               ssm_b_im[j], ssm_c_re[j], ssm_c_im[j], ssm_d[j], ssm_w_glu[j])
            res = mlp_layer(res, g[2], w_up, w_down, g[3], i)
        outs.append(res)
    return outs[0].reshape(1, L, D) if B == 1 else jnp.stack(outs, axis=0)
```

```python
import functools
import math

import jax
import jax.numpy as jnp
from jax import lax
from jax.experimental import pallas as pl
from jax.experimental.pallas import tpu as pltpu

NORM_EPS = 1e-6
LN_EPS = 1e-5
CONV_WIDTH = 31
POOL_WINDOWS = (2, 4, 8, 16)
N_HEADS = 16
N_KV_HEADS = 4
Q_PER_KV = N_HEADS // N_KV_HEADS
IDX_HEADS = 16
IDX_DIM = 64
TOPK_MAX = 256
ROPE_THETA = 500000.0
SSM_GROUP = 16
SSM_STATE = 64

V7X_LANES = 128
V7X_SUBLANES = 8
V7X_VMEM_BYTES = 64 * 1024 * 1024
VMEM_LIMIT = V7X_VMEM_BYTES - 6 * 1024 * 1024

INT_MIN = -(2 ** 31)
NEG_BIG = -1e30
F32 = jnp.float32
BF16 = jnp.bfloat16


def _cparams(*sem):
    return pltpu.CompilerParams(dimension_semantics=sem, vmem_limit_bytes=VMEM_LIMIT)


def _rms(x, g):
    ms = jnp.mean(x * x, axis=-1, keepdims=True)
    return x * lax.rsqrt(ms + NORM_EPS) * g


def _pick_tn(n, cap):
    return max(t for t in range(V7X_LANES, cap + 1, V7X_LANES) if n % t == 0)


def _dot(a, b):
    return jnp.dot(a, b, preferred_element_type=F32)


def _fm_body(*refs, has_norm, has_bias, glu, has_epi, nj, tn):
    it = iter(refs)
    x_ref = next(it)
    gin_ref = next(it) if has_norm else None
    w1_ref = next(it)
    w2_ref = next(it) if glu else None
    b1_ref = next(it) if has_bias else None
    b2_ref = next(it) if (has_bias and glu) else None
    res_ref = next(it) if has_epi else None
    gout_ref = next(it) if has_epi else None
    out_ref = next(it)
    xn_ref = next(it)
    acc_ref = next(it) if has_epi else None

    j = pl.program_id(1)

    @pl.when(j == 0)
    def _():
        x = x_ref[...].astype(F32)
        if has_norm:
            x = _rms(x, gin_ref[...])
        xn_ref[...] = x.astype(BF16)

    xn = xn_ref[...]
    y = _dot(xn, w1_ref[...])
    if has_bias:
        y = y + b1_ref[...]
    if glu:
        y2 = _dot(xn, w2_ref[...])
        if has_bias:
            y2 = y2 + b2_ref[...]
        y = y * jax.nn.sigmoid(y2)
    if not has_epi:
        out_ref[...] = y.astype(out_ref.dtype)
    else:
        acc_ref[j] = y

        @pl.when(j == nj - 1)
        def _():
            ssq = jnp.zeros((acc_ref.shape[1], 1), F32)
            for jj in range(nj):
                a = acc_ref[jj]
                ssq = ssq + jnp.sum(a * a, axis=-1, keepdims=True)
            rs = lax.rsqrt(ssq / (nj * tn) + NORM_EPS)
            for jj in range(nj):
                sl = slice(jj * tn, (jj + 1) * tn)
                out_ref[:, sl] = res_ref[:, sl] + acc_ref[jj] * rs * gout_ref[:, sl]


def fused_matmul(x, w, *, glu=False, gin=None, b=None, res=None, gout=None,
                 out_dtype=F32, tm=512, tn=1024, name="fused_matmul"):
    L, K = x.shape
    N = w.shape[1] // 2 if glu else w.shape[1]
    tn = min(tn, N)
    assert L % tm == 0 and N % tn == 0
    nj = N // tn
    has_norm, has_bias, has_epi = gin is not None, b is not None, res is not None

    args = [x]
    specs = [pl.BlockSpec((tm, K), lambda i, j: (i, 0))]
    if has_norm:
        args.append(gin.reshape(1, K).astype(F32))
        specs.append(pl.BlockSpec((1, K), lambda i, j: (0, 0)))
    args.append(w)
    specs.append(pl.BlockSpec((K, tn), lambda i, j: (0, j)))
    if glu:
        args.append(w)
        specs.append(pl.BlockSpec((K, tn), lambda i, j: (0, j + nj)))
    if has_bias:
        b = b.reshape(1, -1).astype(F32)
        args.append(b)
        specs.append(pl.BlockSpec((1, tn), lambda i, j: (0, j)))
        if glu:
            args.append(b)
            specs.append(pl.BlockSpec((1, tn), lambda i, j: (0, j + nj)))
    scratch = [pltpu.VMEM((tm, K), BF16)]
    if has_epi:
        args += [res, gout.reshape(1, N).astype(F32)]
        specs += [pl.BlockSpec((tm, N), lambda i, j: (i, 0)), pl.BlockSpec((1, N), lambda i, j: (0, 0))]
        out_spec = pl.BlockSpec((tm, N), lambda i, j: (i, 0))
        out_dtype = F32
        scratch.append(pltpu.VMEM((nj, tm, tn), F32))
    else:
        out_spec = pl.BlockSpec((tm, tn), lambda i, j: (i, j))

    body = functools.partial(_fm_body, has_norm=has_norm, has_bias=has_bias, glu=glu,
                             has_epi=has_epi, nj=nj, tn=tn)
    return pl.pallas_call(
        body,
        grid=(L // tm, nj),
        in_specs=specs,
        out_specs=out_spec,
        out_shape=jax.ShapeDtypeStruct((L, N), out_dtype),
        scratch_shapes=scratch,
        compiler_params=_cparams("parallel", "arbitrary"),
        name=name,
    )(*args)


def _mlp_body(res_ref, gin_ref, wu_ref, wd_ref, gout_ref, out_ref, xn_ref, acc_ref, *, nj):
    j = pl.program_id(1)

    @pl.when(j == 0)
    def _():
        xn_ref[...] = _rms(res_ref[...], gin_ref[...]).astype(BF16)
        acc_ref[...] = jnp.zeros_like(acc_ref)

    h = _dot(xn_ref[...], wu_ref[...])
    h = jnp.maximum(h, 0.0)
    acc_ref[...] += _dot((h * h).astype(BF16), wd_ref[...])

    @pl.when(j == nj - 1)
    def _():
        out_ref[...] = res_ref[...] + _rms(acc_ref[...], gout_ref[...])


def mlp_layer(res, gin, w_up, w_down, gout, layer, *, tm=512, tf=1024):
    L, D = res.shape
    F = w_up.shape[2]
    nj = F // tf
    return pl.pallas_call(
        functools.partial(_mlp_body, nj=nj),
        grid=(L // tm, nj),
        in_specs=[
            pl.BlockSpec((tm, D), lambda i, j: (i, 0)),
            pl.BlockSpec((1, D), lambda i, j: (0, 0)),
            pl.BlockSpec((None, D, tf), lambda i, j: (layer, 0, j)),
            pl.BlockSpec((None, tf, D), lambda i, j: (layer, j, 0)),
            pl.BlockSpec((1, D), lambda i, j: (0, 0)),
        ],
        out_specs=pl.BlockSpec((tm, D), lambda i, j: (i, 0)),
        out_shape=jax.ShapeDtypeStruct((L, D), F32),
        scratch_shapes=[pltpu.VMEM((tm, D), BF16), pltpu.VMEM((tm, D), F32)],
        compiler_params=_cparams("parallel", "arbitrary"),
        name="mlp",
    )(res, gin.reshape(1, D), w_up, w_down, gout.reshape(1, D))


CONV_HALO = 32
CONV_ROWS = 32


def _dwconv_body(cur_ref, halo_ref, w_ref, b_ref, lg_ref, lb_ref, out_ref, sh_ref, cv_ref, *, tm):
    i = pl.program_id(0)
    D = cur_ref.shape[1]
    sub, lanes = V7X_SUBLANES, V7X_LANES
    ncol = D // lanes
    row_chunks = range(0, tm, CONV_ROWS)
    halo = halo_ref[...]
    halo = jnp.where(i == 0, jnp.zeros_like(halo), halo)
    for c in range(ncol):
        sh_ref[c, 0, 0:CONV_HALO, :] = halo[:, c * lanes:(c + 1) * lanes]
        sh_ref[c, 0, CONV_HALO:, :] = cur_ref[:, c * lanes:(c + 1) * lanes]
    n = tm + CONV_HALO - sub

    def column(c, carry):
        for j in range(1, sub):
            sh_ref[c, j, sub:, :] = sh_ref[c, 0, sub - j:sub - j + n, :]
        accs = [jnp.zeros((CONV_ROWS, lanes), F32) + b_ref[c] for _ in row_chunks]
        for k in range(CONV_WIDTH):
            a, j = divmod(CONV_WIDTH - 1 - k, sub)
            wk = jnp.broadcast_to(w_ref[c, k:k + 1, :], (CONV_ROWS, lanes))
            for ri, r0 in enumerate(row_chunks):
                start = CONV_HALO + r0 - sub * a
                accs[ri] = accs[ri] + sh_ref[c, j, start:start + CONV_ROWS, :] * wk
        for ri, r0 in enumerate(row_chunks):
            cv_ref[c, r0:r0 + CONV_ROWS, :] = accs[ri]
        return carry

    lax.fori_loop(0, ncol, column, 0)

    s1 = cv_ref[0]
    for c in range(1, ncol):
        s1 = s1 + cv_ref[c]
    mu = jnp.sum(s1, axis=-1, keepdims=True) / D
    s2 = jnp.zeros((tm, lanes), F32)
    for c in range(ncol):
        hc = cv_ref[c] - mu
        s2 = s2 + hc * hc
    rstd = lax.rsqrt(jnp.sum(s2, axis=-1, keepdims=True) / D + LN_EPS)
    for c in range(ncol):
        cs = slice(c * lanes, (c + 1) * lanes)
        y = (cv_ref[c] - mu) * rstd * lg_ref[:, cs] + lb_ref[:, cs]
        out_ref[:, cs] = (y * jax.nn.sigmoid(y)).astype(out_ref.dtype)


def dwconv_ln_swish(h, w_dw, b_dw, ln_g, ln_b, *, tm=256):
    L, D = h.shape
    hb = tm // CONV_HALO
    ncol = D // V7X_LANES
    w_cols = w_dw.reshape(CONV_WIDTH, ncol, V7X_LANES).transpose(1, 0, 2)
    b_cols = b_dw.reshape(ncol, 1, V7X_LANES)
    return pl.pallas_call(
        functools.partial(_dwconv_body, tm=tm),
        grid=(L // tm,),
        in_specs=[
            pl.BlockSpec((tm, D), lambda i: (i, 0)),
            pl.BlockSpec((CONV_HALO, D), lambda i: (jnp.maximum(i * hb - 1, 0), 0)),
            pl.BlockSpec((ncol, CONV_WIDTH, V7X_LANES), lambda i: (0, 0, 0)),
            pl.BlockSpec((ncol, 1, V7X_LANES), lambda i: (0, 0, 0)),
            pl.BlockSpec((1, D), lambda i: (0, 0)),
            pl.BlockSpec((1, D), lambda i: (0, 0)),
        ],
        out_specs=pl.BlockSpec((tm, D), lambda i: (i, 0)),
        out_shape=jax.ShapeDtypeStruct((L, D), BF16),
        scratch_shapes=[pltpu.VMEM((ncol, V7X_SUBLANES, tm + CONV_HALO, V7X_LANES), F32),
                        pltpu.VMEM((ncol, tm, V7X_LANES), F32)],
        compiler_params=_cparams("parallel"),
        name="dwconv_ln_swish",
    )(h, h, w_cols, b_cols, ln_g.reshape(1, D), ln_b.reshape(1, D))


POOL_HALO = 16


def _pool_body(cur_ref, halo_ref, gin_ref, pw_ref, ps_ref, gout_ref, out_ref, ext_ref, mix_ref, *, tm):
    i = pl.program_id(0)
    D = cur_ref.shape[1]
    cg = D // len(POOL_WINDOWS)
    hn = _rms(halo_ref[...], gin_ref[...])
    ext_ref[0:POOL_HALO, :] = jnp.where(i == 0, jnp.zeros_like(hn), hn)
    ext_ref[POOL_HALO:, :] = _rms(cur_ref[...], gin_ref[...])
    t1 = (i * tm + 1 + lax.broadcasted_iota(jnp.int32, (tm, 1), 0)).astype(F32)
    for g, w in enumerate(POOL_WINDOWS):
        cs = slice(g * cg, (g + 1) * cg)
        s = ext_ref[POOL_HALO:POOL_HALO + tm, cs]
        x = s
        for d in range(1, w):
            s = s + ext_ref[POOL_HALO - d:POOL_HALO - d + tm, cs]
        pooled = s / jnp.minimum(t1, float(w)) - x
        mix_ref[:, cs] = _dot(pooled.astype(BF16), pw_ref[g]) * ps_ref[:, cs]
    out_ref[...] = cur_ref[...] + _rms(mix_ref[...], gout_ref[...])


def pool_layer(res, gin, pool_w, pool_scale, gout, *, tm=256):
    L, D = res.shape
    ng = len(POOL_WINDOWS)
    cg = D // ng
    hb = tm // POOL_HALO
    return pl.pallas_call(
        functools.partial(_pool_body, tm=tm),
        grid=(L // tm,),
        in_specs=[
            pl.BlockSpec((tm, D), lambda i: (i, 0)),
            pl.BlockSpec((POOL_HALO, D), lambda i: (jnp.maximum(i * hb - 1, 0), 0)),
            pl.BlockSpec((1, D), lambda i: (0, 0)),
            pl.BlockSpec((ng, cg, cg), lambda i: (0, 0, 0)),
            pl.BlockSpec((1, D), lambda i: (0, 0)),
            pl.BlockSpec((1, D), lambda i: (0, 0)),
        ],
        out_specs=pl.BlockSpec((tm, D), lambda i: (i, 0)),
        out_shape=jax.ShapeDtypeStruct((L, D), F32),
        scratch_shapes=[pltpu.VMEM((tm + POOL_HALO, D), F32), pltpu.VMEM((tm, D), F32)],
        compiler_params=_cparams("parallel"),
        name="pool_layer",
    )(res, res, gin.reshape(1, D), pool_w, pool_scale.reshape(1, D), gout.reshape(1, D))


def _rope_tables(L, width, r):
    half = r // 2
    inv = jnp.power(ROPE_THETA, -2.0 * jnp.arange(half, dtype=F32) / r)
    ang = jnp.arange(L, dtype=F32)[:, None] * inv[None, :]
    cos, sin = jnp.cos(ang), jnp.sin(ang)
    pad = width - r
    cos_h = jnp.concatenate([cos, cos, jnp.ones((L, pad), F32)], axis=1)
    sin_h = jnp.concatenate([-sin, sin, jnp.zeros((L, pad), F32)], axis=1)
    reps = V7X_LANES // width
    return jnp.tile(cos_h, (1, reps)), jnp.tile(sin_h, (1, reps))


def _rope128(x, cos_t, sin_t, half, width):
    lane = lax.broadcasted_iota(jnp.int32, x.shape, 1) % width
    fwd = pltpu.roll(x, V7X_LANES - half, axis=1)
    bwd = pltpu.roll(x, half, axis=1)
    partner = jnp.where(lane < half, fwd, bwd)
    return x * cos_t + partner * sin_t


def _rope_body(p_ref, cq_ref, sq_ref, ci_ref, si_ref, q_ref, k_ref, v_ref, qi_ref, ki_ref, wi_ref,
               *, dq, dkv, dqi, q_scale, wi_scale):
    cq, sq, ci, si = cq_ref[...], sq_ref[...], ci_ref[...], si_ref[...]
    hd = V7X_LANES
    for h in range(dq // hd):
        x = p_ref[:, h * hd:(h + 1) * hd]
        q_ref[:, h * hd:(h + 1) * hd] = (_rope128(x, cq, sq, 16, hd) * q_scale).astype(BF16)
    o = dq
    for h in range(dkv // hd):
        x = p_ref[:, o + h * hd:o + (h + 1) * hd]
        k_ref[:, h * hd:(h + 1) * hd] = _rope128(x, cq, sq, 16, hd).astype(BF16)
    o += dkv
    v_ref[...] = p_ref[:, o:o + dkv].astype(BF16)
    o += dkv
    for h in range(dqi // hd):
        x = p_ref[:, o + h * hd:o + (h + 1) * hd]
        qi_ref[:, h * hd:(h + 1) * hd] = _rope128(x, ci, si, 8, IDX_DIM).astype(BF16)
    o += dqi
    x = p_ref[:, o:o + hd]
    xr = _rope128(x, ci, si, 8, IDX_DIM)
    ki_ref[...] = xr[:, :IDX_DIM].astype(BF16)
    wi_ref[...] = x[:, IDX_DIM:IDX_DIM + IDX_HEADS] * wi_scale


def rope_split(proj, dq, dkv, dqi, *, tm=256):
    L, W = proj.shape
    cq, sq = _rope_tables(L, V7X_LANES, 32)
    ci, si = _rope_tables(L, IDX_DIM, 16)
    tab = pl.BlockSpec((tm, V7X_LANES), lambda i: (i, 0))
    row = lambda n: pl.BlockSpec((tm, n), lambda i: (i, 0))
    body = functools.partial(_rope_body, dq=dq, dkv=dkv, dqi=dqi, q_scale=V7X_LANES ** -0.5 * LOG2E,
                             wi_scale=IDX_HEADS ** -0.5 * IDX_DIM ** -0.5)
    return pl.pallas_call(
        body,
        grid=(L // tm,),
        in_specs=[row(W), tab, tab, tab, tab],
        out_specs=[row(dq), row(dkv), row(dkv), row(dqi), row(IDX_DIM), row(IDX_HEADS)],
        out_shape=[
            jax.ShapeDtypeStruct((L, dq), BF16), jax.ShapeDtypeStruct((L, dkv), BF16),
            jax.ShapeDtypeStruct((L, dkv), BF16), jax.ShapeDtypeStruct((L, dqi), BF16),
            jax.ShapeDtypeStruct((L, IDX_DIM), BF16), jax.ShapeDtypeStruct((L, IDX_HEADS), F32),
        ],
        compiler_params=_cparams("parallel"),
        name="rope_split",
    )(proj, cq, sq, ci, si)


IDX_TQ = 128
IDX_CK = 512


def _sortable(x):
    b = pltpu.bitcast(x + 0.0, jnp.int32)
    return jnp.where(b < 0, b ^ jnp.int32(0x7FFFFFFF), b)


def _index_body(qi_ref, kit_ref, wi_ref, mask_ref, keys_ref, jmax_ref, qs_ref, *, topk, nck):
    i = pl.program_id(0)
    tq, ck = IDX_TQ, IDX_CK
    n_act = (i * tq + tq + ck - 1) // ck
    tpos = i * tq + lax.broadcasted_iota(jnp.int32, (tq, 1), 0)
    wi = wi_ref[...]
    for h in range(IDX_HEADS):
        qs_ref[h * tq:(h + 1) * tq, :] = qi_ref[:, h * IDX_DIM:(h + 1) * IDX_DIM]

    def fill(c, carry):
        sc = _dot(qs_ref[...], kit_ref[c])
        acc = jnp.zeros((tq, ck), F32)
        for h in range(IDX_HEADS):
            acc = acc + jnp.maximum(sc[h * tq:(h + 1) * tq], 0.0) * wi[:, h:h + 1]
        spos = c * ck + lax.broadcasted_iota(jnp.int32, (tq, ck), 1)
        keys_ref[c] = jnp.where(spos <= tpos, _sortable(acc), INT_MIN)
        return carry

    lax.fori_loop(0, n_act, fill, 0)

    lanes = V7X_LANES
    lane_id = lax.broadcasted_iota(jnp.int32, (tq, lanes), 1)

    def count(pred):
        def step(c, part):
            for l0 in range(0, ck, lanes):
                part = part + jnp.where(pred(keys_ref[c, :, l0:l0 + lanes], c * ck + l0), 1, 0)
            return part
        part = lax.fori_loop(0, n_act, step, jnp.zeros((tq, lanes), jnp.int32))
        return jnp.sum(part, axis=-1, keepdims=True)

    cnt0 = count(lambda kg, p0: kg >= 0)
    thr = jnp.broadcast_to(jnp.where(cnt0 >= topk, jnp.int32(0), jnp.int32(INT_MIN)), (tq, lanes))

    def bit_step(b, thr):
        cand = thr | (jnp.int32(1) << (30 - b))
        return jnp.where(count(lambda kg, p0: kg >= cand) >= topk, cand, thr)

    thr = lax.fori_loop(0, 31, bit_step, thr)

    need = topk - count(lambda kg, p0: kg > thr)
    n_eq = count(lambda kg, p0: kg == thr)
    jmax_ref[...] = jnp.full(jmax_ref.shape, nck * ck, jnp.int32)

    @pl.when(jnp.max(n_eq - need) > 0)
    def _():
        nbits = (nck * ck - 1).bit_length()

        def jstep(b, j):
            cand = j & ~(jnp.int32(1) << (nbits - 1 - b))
            n_upto = count(lambda kg, p0: (kg == thr) & (p0 + lane_id <= cand))
            return jnp.where(n_upto >= need, cand, j)

        j0 = jnp.full((tq, lanes), (1 << nbits) - 1, jnp.int32)
        jmax_ref[...] = lax.fori_loop(0, nbits, jstep, j0)

    jmax = jmax_ref[...]
    for c in range(nck):
        @pl.when(c < n_act)
        def _():
            for l0 in range(0, ck, lanes):
                kg = keys_ref[c, :, l0:l0 + lanes]
                sel = ((kg > thr) | ((kg == thr) & (c * ck + l0 + lane_id <= jmax))) & (kg > INT_MIN)
                mask_ref[:, c * ck + l0:c * ck + l0 + lanes] = jnp.where(sel, 1, 0).astype(jnp.int8)

        @pl.when(c >= n_act)
        def _():
            mask_ref[:, c * ck:(c + 1) * ck] = jnp.zeros((tq, ck), jnp.int8)


def index_mask(qi, ki, wi, topk):
    L = qi.shape[0]
    nck = L // IDX_CK
    kit = ki.reshape(nck, IDX_CK, IDX_DIM).transpose(0, 2, 1)
    return pl.pallas_call(
        functools.partial(_index_body, topk=topk, nck=nck),
        grid=(L // IDX_TQ,),
        in_specs=[
            pl.BlockSpec((IDX_TQ, IDX_HEADS * IDX_DIM), lambda i: (i, 0)),
            pl.BlockSpec((nck, IDX_DIM, IDX_CK), lambda i: (0, 0, 0)),
            pl.BlockSpec((IDX_TQ, IDX_HEADS), lambda i: (i, 0)),
        ],
        out_specs=pl.BlockSpec((IDX_TQ, L), lambda i: (i, 0)),
        out_shape=jax.ShapeDtypeStruct((L, L), jnp.int8),
        scratch_shapes=[pltpu.VMEM((nck, IDX_TQ, IDX_CK), jnp.int32), pltpu.VMEM((IDX_TQ, V7X_LANES), jnp.int32),
                        pltpu.VMEM((IDX_HEADS * IDX_TQ, IDX_DIM), BF16)],
        compiler_params=_cparams("parallel"),
        name="index_mask",
    )(qi, kit, wi)


ATT_TQ = 512
ATT_TK = 512
ATT_RB = 32
ATT_PV_ROWS = 128
LOG2E = 1.4426950408889634


def _attn_body(qb_ref, kc_ref, q_ref, k_ref, v_ref, mask_ref, o_ref, m_ref, l_ref, acc_ref, s_ref, p_ref, a_ref,
               bias_ref):
    step = pl.program_id(0)
    i = qb_ref[step]
    c = kc_ref[step]
    tq, tk, hd, rb = ATT_TQ, ATT_TK, V7X_LANES, ATT_RB
    rows = Q_PER_KV * tq
    nlc = tk // hd

    @pl.when(c == 0)
    def _():
        m_ref[...] = jnp.full_like(m_ref, NEG_BIG)
        l_ref[...] = jnp.zeros_like(l_ref)
        acc_ref[...] = jnp.zeros_like(acc_ref)

    bias_ref[...] = jnp.where(mask_ref[...].astype(F32) > 0.5, 0.0, NEG_BIG)
    for n in range(N_KV_HEADS):
        sb, pb, ab = s_ref.at[n % 2], p_ref.at[n % 2], a_ref.at[n % 2]
        qs = jnp.concatenate(
            [q_ref[:, (n * Q_PER_KV + g) * hd:(n * Q_PER_KV + g + 1) * hd] for g in range(Q_PER_KV)], axis=0)
        kn = k_ref[:, n * hd:(n + 1) * hd]
        s = lax.dot_general(qs, kn, (((1,), (1,)), ((), ())), preferred_element_type=F32)
        sb[...] = (s.reshape(Q_PER_KV, tq, tk) + bias_ref[...][None]).reshape(rows, tk)

        for r0 in range(0, rows, rb):
            hr = slice(n * rows + r0, n * rows + r0 + rb)
            x = sb[r0:r0 + rb, :]
            mx = x[:, 0:hd]
            for j in range(1, nlc):
                mx = jnp.maximum(mx, x[:, j * hd:(j + 1) * hd])
            m_prev = m_ref[hr, :]
            m_new = jnp.maximum(m_prev, jnp.max(mx, axis=-1, keepdims=True))
            ab[r0:r0 + rb, :] = jnp.exp2(m_prev - m_new)
            m_ref[hr, :] = m_new

        def exp_rows(t0):
            for r0 in range(t0, t0 + ATT_PV_ROWS, rb):
                hr = slice(n * rows + r0, n * rows + r0 + rb)
                m_new = m_ref[hr, :]
                psum = jnp.zeros((rb, hd), F32)
                for j in range(nlc):
                    p = jnp.exp2(sb[r0:r0 + rb, j * hd:(j + 1) * hd] - m_new)
                    psum = psum + p
                    pb[r0:r0 + rb, j * hd:(j + 1) * hd] = p.astype(BF16)
                l_ref[hr, :] = ab[r0:r0 + rb, :] * l_ref[hr, :] + psum

        def pv_rows(t0):
            hr = slice(n * rows + t0, n * rows + t0 + ATT_PV_ROWS)
            tr = slice(t0, t0 + ATT_PV_ROWS)
            acc_ref[hr, :] = ab[tr, :] * acc_ref[hr, :] + _dot(pb[tr, :], v_ref[:, n * hd:(n + 1) * hd])

        exp_rows(0)
        for t0 in range(ATT_PV_ROWS, rows, ATT_PV_ROWS):
            exp_rows(t0)
            pv_rows(t0 - ATT_PV_ROWS)
        pv_rows(rows - ATT_PV_ROWS)

    @pl.when(c == (i * tq + tq - 1) // tk)
    def _():
        for h in range(N_HEADS):
            rs = slice(h * tq, (h + 1) * tq)
            l = jnp.sum(l_ref[rs, :], axis=-1, keepdims=True)
            o_ref[:, h * hd:(h + 1) * hd] = (acc_ref[rs, :] / l).astype(o_ref.dtype)


def masked_attention(q, k, v, mask):
    L, dq = q.shape
    dkv = k.shape[1]
    tq, tk = ATT_TQ, ATT_TK
    pairs = [(i, c) for i in range(L // tq) for c in range((i * tq + tq - 1) // tk + 1)]
    qb = jnp.asarray([p[0] for p in pairs], jnp.int32)
    kc = jnp.asarray([p[1] for p in pairs], jnp.int32)
    stat = pltpu.VMEM((N_HEADS * tq, V7X_LANES), F32)
    grid_spec = pltpu.PrefetchScalarGridSpec(
        num_scalar_prefetch=2,
        grid=(len(pairs),),
        in_specs=[
            pl.BlockSpec((tq, dq), lambda s, qb, kc: (qb[s], 0)),
            pl.BlockSpec((tk, dkv), lambda s, qb, kc: (kc[s], 0)),
            pl.BlockSpec((tk, dkv), lambda s, qb, kc: (kc[s], 0)),
            pl.BlockSpec((tq, tk), lambda s, qb, kc: (qb[s], kc[s])),
        ],
        out_specs=pl.BlockSpec((tq, dq), lambda s, qb, kc: (qb[s], 0)),
        scratch_shapes=[stat, stat, stat,
                        pltpu.VMEM((2, Q_PER_KV * tq, tk), F32), pltpu.VMEM((2, Q_PER_KV * tq, tk), BF16),
                        pltpu.VMEM((2, Q_PER_KV * tq, V7X_LANES), F32), pltpu.VMEM((tq, tk), F32)],
    )
    return pl.pallas_call(
        _attn_body,
        grid_spec=grid_spec,
        out_shape=jax.ShapeDtypeStruct((L, dq), BF16),
        compiler_params=_cparams("arbitrary"),
        name="masked_attention",
    )(qb, kc, q, k, v, mask)


S5_OCT = V7X_SUBLANES
S5_PAD = 8


def _s5_body(res_ref, gin_ref, b_ref, c_ref, ar_ref, ai_ref, d_ref, y_ref, sr_ref, si_ref, hr_ref, hi_ref,
             u_ref, *, tc, nm):
    t_blk = pl.program_id(0)
    pitch = tc + S5_PAD
    uo = u_ref.shape[1] // S5_OCT
    so = nm * V7X_LANES

    @pl.when(t_blk == 0)
    def _():
        hr_ref[...] = jnp.zeros_like(hr_ref)
        hi_ref[...] = jnp.zeros_like(hi_ref)

    u_ref[...] = _rms(res_ref[...], gin_ref[...])
    for s in range(S5_OCT):
        bu = _dot(u_ref[:, s * uo:(s + 1) * uo].astype(BF16), b_ref[s])
        for m in range(nm):
            sr_ref[m, s * pitch:s * pitch + tc, :] = bu[:, m * V7X_LANES:(m + 1) * V7X_LANES]
            si_ref[m, s * pitch:s * pitch + tc, :] = bu[:, so + m * V7X_LANES:so + (m + 1) * V7X_LANES]

    ar = [ar_ref[m] for m in range(nm)]
    ai = [ai_ref[m] for m in range(nm)]

    def step(t, carry):
        hr, hi = carry
        nr, ni = [], []
        for m in range(nm):
            idx = (m, pl.ds(t, S5_OCT, stride=pitch), slice(None))
            r = ar[m] * hr[m] - ai[m] * hi[m] + sr_ref[idx]
            im = ar[m] * hi[m] + ai[m] * hr[m] + si_ref[idx]
            sr_ref[idx] = r
            si_ref[idx] = im
            nr.append(r)
            ni.append(im)
        return tuple(nr), tuple(ni)

    init = (tuple(hr_ref[m] for m in range(nm)), tuple(hi_ref[m] for m in range(nm)))
    hr, hi = lax.fori_loop(0, tc, step, init)
    for m in range(nm):
        hr_ref[m] = hr[m]
        hi_ref[m] = hi[m]

    for s in range(S5_OCT):
        rows = slice(s * pitch, s * pitch + tc)
        h = jnp.concatenate([sr_ref[m, rows, :] for m in range(nm)] + [si_ref[m, rows, :] for m in range(nm)], axis=1)
        cs = slice(s * uo, (s + 1) * uo)
        y = _dot(h.astype(BF16), c_ref[s]) + d_ref[:, cs] * u_ref[:, cs]
        y_ref[:, cs] = jax.nn.gelu(y).astype(y_ref.dtype)


def s5_core(res, gin, lam_re, lam_im, log_dt, b_re, b_im, c_re, c_im, d_skip, *, tc=256):
    L, D = res.shape
    G, P = lam_re.shape
    go = G // S5_OCT
    so = go * P
    nm = so // V7X_LANES
    uo = go * SSM_GROUP
    dt = jnp.exp(log_dt)[:, None]
    mag = jnp.exp(lam_re * dt)
    ar = mag * jnp.cos(lam_im * dt)
    ai = mag * jnp.sin(lam_im * dt)
    den = lam_re * lam_re + lam_im * lam_im
    cr = ((ar - 1.0) * lam_re + ai * lam_im) / den
    ci = (ai * lam_re - (ar - 1.0) * lam_im) / den
    bbr = cr[..., None] * b_re - ci[..., None] * b_im
    bbi = cr[..., None] * b_im + ci[..., None] * b_re
    eye = jnp.eye(go, dtype=F32)

    def blockdiag_in(bb):
        bb = bb.reshape(S5_OCT, go, P, SSM_GROUP)
        return jnp.einsum('sgpc,gh->sgchp', bb, eye).reshape(S5_OCT, uo, so)

    def blockdiag_out(cc):
        cc = cc.reshape(S5_OCT, go, SSM_GROUP, P)
        return jnp.einsum('sgcp,gh->sgphc', cc, eye).reshape(S5_OCT, so, uo)

    bmat = jnp.concatenate([blockdiag_in(bbr), blockdiag_in(bbi)], axis=2).astype(BF16)
    cmat = jnp.concatenate([blockdiag_out(c_re), blockdiag_out(-c_im)], axis=1).astype(BF16)
    a_r = ar.reshape(S5_OCT, nm, V7X_LANES).transpose(1, 0, 2)
    a_i = ai.reshape(S5_OCT, nm, V7X_LANES).transpose(1, 0, 2)

    pitch = tc + S5_PAD
    const3 = lambda t: (0, 0, 0)
    return pl.pallas_call(
        functools.partial(_s5_body, tc=tc, nm=nm),
        grid=(L // tc,),
        in_specs=[
            pl.BlockSpec((tc, D), lambda t: (t, 0)),
            pl.BlockSpec((1, D), lambda t: (0, 0)),
            pl.BlockSpec((S5_OCT, uo, 2 * so), const3, pipeline_mode=pl.Buffered(1)),
            pl.BlockSpec((S5_OCT, 2 * so, uo), const3, pipeline_mode=pl.Buffered(1)),
            pl.BlockSpec((nm, S5_OCT, V7X_LANES), const3),
            pl.BlockSpec((nm, S5_OCT, V7X_LANES), const3),
            pl.BlockSpec((1, D), lambda t: (0, 0)),
        ],
        out_specs=pl.BlockSpec((tc, D), lambda t: (t, 0)),
        out_shape=jax.ShapeDtypeStruct((L, D), BF16),
        scratch_shapes=[
            pltpu.VMEM((nm, S5_OCT * pitch, V7X_LANES), F32),
            pltpu.VMEM((nm, S5_OCT * pitch, V7X_LANES), F32),
            pltpu.VMEM((nm, S5_OCT, V7X_LANES), F32),
            pltpu.VMEM((nm, S5_OCT, V7X_LANES), F32),
            pltpu.VMEM((tc, D), F32),
        ],
        compiler_params=_cparams("arbitrary"),
        name="s5_core",
    )(res, gin.reshape(1, D), bmat, cmat, a_r, a_i, d_skip.reshape(1, D))


def conv_layer(res, g0, g1, w_in, b_in, w_dw, b_dw, ln_g, ln_b, w_out):
    h = fused_matmul(res, w_in.astype(BF16), glu=True, gin=g0, b=b_in, name="conv_in_glu")
    h = dwconv_ln_swish(h, w_dw, b_dw, ln_g, ln_b)
    return fused_matmul(h, w_out.astype(BF16), res=res, gout=g1, name="conv_out")


def attn_layer(res, g0, g1, w_in, w_out):
    L, D = res.shape
    dq = D
    dkv = N_KV_HEADS * (D // N_HEADS)
    dqi = IDX_HEADS * IDX_DIM
    width = w_in.shape[1]
    pad = (-width) % V7X_LANES
    w_in = jnp.pad(w_in.astype(BF16), ((0, 0), (0, pad)))
    proj = fused_matmul(res, w_in, gin=g0, tn=_pick_tn(width + pad, 1536), name="attn_in")
    q, k, v, qi, ki, wi = rope_split(proj, dq, dkv, dqi)
    mask = index_mask(qi, ki, wi, min(TOPK_MAX, L // 4))
    o = masked_attention(q, k, v, mask)
    return fused_matmul(o, w_out.astype(BF16), res=res, gout=g1, name="attn_out")


def s5_layer(res, g0, g1, lam_re, lam_im, log_dt, b_re, b_im, c_re, c_im, d_skip, w_glu):
    y = s5_core(res, g0, lam_re, lam_im, log_dt, b_re, b_im, c_re, c_im, d_skip)
    return fused_matmul(y, w_glu.astype(BF16), glu=True, res=res, gout=g1, name="s5_glu")


def kernel(x, norm_gains, mlp_w_up, mlp_w_down, conv_w_in, conv_b_in, conv_w_dw, conv_b_dw, conv_ln_g, conv_ln_b, conv_w_out, pool_w, pool_scale, attn_w_in, attn_w_out, ssm_lambda_re, ssm_lambda_im, ssm_log_dt, ssm_b_re, ssm_b_im, ssm_c_re, ssm_c_im, ssm_d, ssm_w_glu):
    B, L, D = x.shape
    depth = norm_gains.shape[0]
    w_up = mlp_w_up.astype(BF16)
    w_down = mlp_w_down.astype(BF16)
    outs = []
    for b in range(B):
        res = x.reshape(L, D) if B == 1 else x[b]
        for i in range(depth):
            m, j = i % 4, i // 4
            g = norm_gains[i]
            if m == 0:
                res = conv_layer(res, g[0], g[1], conv_w_in[j], conv_b_in[j], conv_w_dw[j], conv_b_dw[j],
                                 conv_ln_g[j], conv_ln_b[j], conv_w_out[j])
            elif m == 1:
                res = pool_layer(res, g[0], pool_w[j].astype(BF16), pool_scale[j], g[1])
            elif m == 2:
                res = attn_layer(res, g[0], g[1], attn_w_in[j], attn_w_out[j])
            else:
                res = s5_layer(res, g[0], g[1], ssm_lambda_re[j], ssm_lambda_im[j], ssm_log_dt[j], ssm_b_re[j],
                               ssm_b_im[j], ssm_c_re[j], ssm_c_im[j], ssm_d[j], ssm_w_glu[j])
            res = mlp_layer(res, g[2], w_up, w_down, g[3], i)
        outs.append(res)
    return outs[0].reshape(1, L, D) if B == 1 else jnp.stack(outs, axis=0)
```
